```python
import math
import jax, jax.numpy as jnp
from jax import lax
import numpy as np

D_MODEL = 2048
BATCH = 16
SEQ = 256
DEPTH = 2
DEC_BATCH = 8
DEC_SEQ = 1024
PAST_LEN = 256

GRID_W = 64
D_MIX = D_MODEL
D_SSM = D_MIX // 2
D_CONV = D_MIX - D_SSM
SSM_H = 16
SSM_GROUPS = D_SSM // SSM_H
SSM_STATE = 64
N_DIR = 2
CONV_W = 3
CONV_HEADS = 16
D_IN = D_SSM + 3 * D_CONV
D_FF = ((8 * D_MODEL + 3 * 256 - 1) // (3 * 256)) * 256
N_MOD = 6
EPS = 1e-6
LAM_RE_MAX = -1e-4
DT_MIN = 1e-3
DT_MAX = 1e-1

kernel_name = 'hymba_s5_shortconv_prefix_dit_step'


def rms_norm(x, g):
    xf = x.astype(jnp.float32)
    y = xf * lax.rsqrt(jnp.mean(xf * xf, axis=-1, keepdims=True) + EPS)
    return (y * g.astype(jnp.float32)).astype(x.dtype)


def modulation(cond, w, b):
    m = (jax.nn.silu(cond) @ w + b).reshape(cond.shape[0], N_MOD, 1, D_MODEL)
    return [m[:, i] for i in range(N_MOD)]


def _scan_combine(left, right):
    a_l, b_l = left
    a_r, b_r = right
    return a_r * a_l, a_r * b_l + b_r


def s5_direction(u, lam_re, lam_im, log_dt, b_re, b_im, c_re, c_im, s0, reverse):
    f32 = jnp.float32
    lam = lax.complex(jnp.minimum(lam_re.astype(f32), LAM_RE_MAX), lam_im.astype(f32))
    dt = jnp.exp(log_dt.astype(f32))[:, None]
    a_bar = jnp.exp(lam * dt)
    b_bar = ((a_bar - 1.0) / lam)[..., None] * lax.complex(b_re.astype(f32), b_im.astype(f32))
    bu = lax.complex(jnp.einsum('blgh,gph->blgp', u, b_bar.real),
                     jnp.einsum('blgh,gph->blgp', u, b_bar.imag))
    if s0 is not None:
        first = -1 if reverse else 0
        bu = bu.at[:, first].add(a_bar * s0)
    a = jnp.broadcast_to(a_bar, bu.shape)
    _, xs = lax.associative_scan(_scan_combine, (a, bu), axis=1, reverse=reverse)
    y = (jnp.einsum('blgp,ghp->blgh', xs.real, c_re.astype(f32))
         - jnp.einsum('blgp,ghp->blgh', xs.imag, c_im.astype(f32)))
    final = xs[:, 0] if reverse else xs[:, -1]
    return y, final


def conv3(z, w, b):
    zp = jnp.pad(z, [(0, 0)] * (z.ndim - 2) + [(1, 1), (0, 0)])
    return zp[..., :-2, :] * w[0] + zp[..., 1:-1, :] * w[1] + zp[..., 2:, :] * w[2] + b


def token_mixers(h, w_in, lam_re, lam_im, log_dt, b_re, b_im, c_re, c_im, d_skip, w_glu,
                 conv_w, conv_b, w_out, s0, on_grid):
    bsz, length, _ = h.shape
    z = h @ w_in
    u, gb, gc, v = jnp.split(z, [D_SSM, D_SSM + D_CONV, D_SSM + 2 * D_CONV], axis=-1)
    uf = u.astype(jnp.float32)
    ug = uf.reshape(bsz, length, SSM_GROUPS, SSM_H)
    ys, finals = [], []
    for d in range(N_DIR):
        y_d, f_d = s5_direction(ug, lam_re[d], lam_im[d], log_dt[d], b_re[d], b_im[d],
                                c_re[d], c_im[d], None if s0 is None else s0[d], d == 1)
        ys.append(y_d)
        finals.append(f_d)
    y = (ys[0] + ys[1]).reshape(bsz, length, D_SSM) + d_skip.astype(jnp.float32) * uf
    y = jax.nn.gelu(y).astype(h.dtype)
    y_ssm = y * jax.nn.sigmoid(y @ w_glu)
    zc = gc * v
    if on_grid:
        rows = length // GRID_W
        zc = conv3(zc.reshape(bsz, rows, GRID_W, D_CONV), conv_w, conv_b).reshape(bsz, length, D_CONV)
    else:
        zc = conv3(zc, conv_w, conv_b)
    y_conv = gb * zc
    out = jnp.concatenate([y_ssm, y_conv], axis=-1) @ w_out
    return out, finals


def setup_inputs(seed: int = 0) -> dict:
    key = jax.random.key(seed)
    ks = jax.random.split(key, 32)
    f32 = jnp.float32

    def nrm(k, shape, scale):
        return jax.random.normal(k, shape, f32) * scale

    gs = (DEPTH, N_DIR, SSM_GROUPS)
    lam_im_base = jnp.pi * jnp.arange(SSM_STATE, dtype=f32)
    return {
        'x_prompt': nrm(ks[0], (BATCH, SEQ, D_MODEL), 1.0),
        'x_sample': nrm(ks[1], (DEC_BATCH, DEC_SEQ, D_MODEL), 1.0),
        'state_ssm': nrm(ks[2], (DEC_BATCH, DEPTH, N_DIR, 2, SSM_GROUPS, SSM_STATE), 0.5),
        'c': nrm(ks[3], (DEC_BATCH, D_MODEL), 1.0),
        'c_ctx': nrm(ks[4], (D_MODEL,), 1.0),
        'w_ada': nrm(ks[5], (DEPTH, D_MODEL, N_MOD * D_MODEL), 0.5 * D_MODEL ** -0.5),
        'b_ada': nrm(ks[6], (DEPTH, N_MOD * D_MODEL), 0.02),
        'g_mix': 1.0 + nrm(ks[7], (DEPTH, D_MODEL), 0.02),
        'w_in': nrm(ks[8], (DEPTH, D_MODEL, D_IN), D_MODEL ** -0.5),
        'ssm_lam_re': -0.5 + nrm(ks[9], gs + (SSM_STATE,), 0.01),
        'ssm_lam_im': lam_im_base + nrm(ks[10], gs + (SSM_STATE,), 0.01),
        'ssm_log_dt': jax.random.uniform(ks[11], gs, f32, math.log(DT_MIN), math.log(DT_MAX)),
        'ssm_b_re': nrm(ks[12], gs + (SSM_STATE, SSM_H), (2.0 * SSM_H) ** -0.5),
        'ssm_b_im': nrm(ks[13], gs + (SSM_STATE, SSM_H), (2.0 * SSM_H) ** -0.5),
        'ssm_c_re': nrm(ks[14], gs + (SSM_H, SSM_STATE), (2.0 * SSM_STATE) ** -0.5),
        'ssm_c_im': nrm(ks[15], gs + (SSM_H, SSM_STATE), (2.0 * SSM_STATE) ** -0.5),
        'ssm_d': nrm(ks[16], (DEPTH, D_SSM), 0.5),
        'w_glu': nrm(ks[17], (DEPTH, D_SSM, D_SSM), D_SSM ** -0.5),
        'conv_w': nrm(ks[18], (DEPTH, CONV_W, D_CONV), CONV_W ** -0.5),
        'conv_b': nrm(ks[19], (DEPTH, D_CONV), 0.02),
        'w_out': nrm(ks[20], (DEPTH, D_MIX, D_MODEL), D_MIX ** -0.5),
        'g_ffn': 1.0 + nrm(ks[21], (DEPTH, D_MODEL), 0.02),
        'w_gate': nrm(ks[22], (DEPTH, D_MODEL, D_FF), D_MODEL ** -0.5),
        'w_up': nrm(ks[23], (DEPTH, D_MODEL, D_FF), D_MODEL ** -0.5),
        'w_down': nrm(ks[24], (DEPTH, D_FF, D_MODEL), D_FF ** -0.5),
        'g_final': 1.0 + nrm(ks[25], (D_MODEL,), 0.02),
    }


def reference(x_prompt, x_sample, state_ssm, c, c_ctx, w_ada, b_ada, g_mix, w_in,
              ssm_lam_re, ssm_lam_im, ssm_log_dt, ssm_b_re, ssm_b_im, ssm_c_re, ssm_c_im,
              ssm_d, w_glu, conv_w, conv_b, w_out, g_ffn, w_gate, w_up, w_down, g_final):

    def layer(x, cond, l, s0, on_grid):
        sh1, sc1, g1, sh2, sc2, g2 = modulation(cond, w_ada[l], b_ada[l])
        h = rms_norm(x, g_mix[l]) * (1.0 + sc1) + sh1
        out, finals = token_mixers(h, w_in[l], ssm_lam_re[l], ssm_lam_im[l], ssm_log_dt[l],
                                   ssm_b_re[l], ssm_b_im[l], ssm_c_re[l], ssm_c_im[l], ssm_d[l],
                                   w_glu[l], conv_w[l], conv_b[l], w_out[l], s0, on_grid)
        x = x + g1 * out
        h = rms_norm(x, g_ffn[l]) * (1.0 + sc2) + sh2
        x = x + g2 * ((jax.nn.silu(h @ w_gate[l]) * (h @ w_up[l])) @ w_down[l])
        return x, finals

    ctx_cond = c_ctx[None, :]
    xp = x_prompt
    per_layer_states = []
    for l in range(DEPTH):
        xp, finals = layer(xp, ctx_cond, l, None, False)
        per_layer_states.append(jnp.stack([jnp.stack([f.real, f.imag], axis=1) for f in finals], axis=1))
    y_prompt = rms_norm(xp, g_final)
    new_state_ssm = jnp.stack(per_layer_states, axis=1).astype(x_prompt.dtype)

    xs = x_sample
    for l in range(DEPTH):
        st = state_ssm[:, l].astype(jnp.float32)
        s0 = [lax.complex(st[:, d, 0], st[:, d, 1]) for d in range(N_DIR)]
        xs, _ = layer(xs, c, l, s0, True)
    y_sample = rms_norm(xs, g_final)

    return (y_prompt, y_sample, new_state_ssm)
```

```python
import functools

import jax
import jax.numpy as jnp
from jax import lax
from jax.experimental import pallas as pl
from jax.experimental.pallas import tpu as pltpu

D_MODEL = 2048
DEPTH = 2
GRID_W = 64
D_SSM = 1024
D_CONV = 1024
SSM_H = 16
SSM_GROUPS = 64
SSM_STATE = 64
N_DIR = 2
D_FF = 5632
N_MOD = 6
EPS = 1e-6
LAM_RE_MAX = -1e-4

SUBLANES = 8
LANES = 128
GROUPS_PER_BLOCK = LANES // SSM_H
N_GROUP_BLOCKS = SSM_GROUPS // GROUPS_PER_BLOCK
PAIRS = GROUPS_PER_BLOCK // 2
STATE_LANES = GROUPS_PER_BLOCK * 2 * SSM_STATE
SCAN_STEPS = 16
VMEM_LIMIT = 56 * 1024 * 1024

F32 = jnp.float32
BF16 = jnp.bfloat16


def _params(*sem):
    return pltpu.CompilerParams(dimension_semantics=sem, vmem_limit_bytes=VMEM_LIMIT)


def _rows8(a):
    return a.reshape(a.shape[0] // SUBLANES, SUBLANES, a.shape[1])


def _norm_mod(x, gain, scale, shift):
    ms = jnp.mean(x * x, axis=-1, keepdims=True)
    y = x * lax.rsqrt(ms + EPS) * gain
    h = _rows8(y) * (1.0 + scale)[None] + shift[None]
    return h.reshape(x.shape)


def _mod_kernel(cond_ref, w_ref, b_ref, o_ref):
    s = jax.nn.silu(cond_ref[...]).astype(BF16)
    o_ref[0] = jnp.dot(s, w_ref[0].astype(BF16), preferred_element_type=F32) + b_ref[0]


def _modulation(cond, w_ada, b_ada):
    tn = 1024
    n = N_MOD * D_MODEL
    return pl.pallas_call(
        _mod_kernel,
        grid=(DEPTH, n // tn),
        in_specs=[
            pl.BlockSpec((16, D_MODEL), lambda l, j: (0, 0)),
            pl.BlockSpec((1, D_MODEL, tn), lambda l, j: (l, 0, j)),
            pl.BlockSpec((1, 1, tn), lambda l, j: (l, 0, j)),
        ],
        out_specs=pl.BlockSpec((1, 16, tn), lambda l, j: (l, 0, j)),
        out_shape=jax.ShapeDtypeStruct((DEPTH, 16, n), F32),
        compiler_params=_params("parallel", "parallel"),
        name="adaln_modulation",
    )(cond, w_ada, b_ada.reshape(DEPTH, 1, n))


def _inproj_kernel(x_ref, mod_ref, g_ref, w_ref, u_ref, gb_ref, zc_ref, h_scr):
    @pl.when(pl.program_id(1) == 0)
    def _():
        h = _norm_mod(x_ref[...], g_ref[...], mod_ref[1], mod_ref[0])
        h_scr[...] = h.astype(BF16)

    z = jnp.dot(h_scr[...], w_ref[...], preferred_element_type=F32)
    q = z.shape[1] // 4
    u_ref[...] = z[:, :q]
    gb_ref[...] = z[:, q:2 * q]
    zc_ref[...] = z[:, 2 * q:3 * q] * z[:, 3 * q:]


def _inproj(x, mod, g_mix, w_in_tiled, tm):
    r = x.shape[0]
    tn = 1024
    q = tn // 4
    out = jax.ShapeDtypeStruct((r, D_SSM), F32)
    return pl.pallas_call(
        _inproj_kernel,
        grid=(r // tm, 4 * D_SSM // tn),
        in_specs=[
            pl.BlockSpec((tm, D_MODEL), lambda i, j: (i, 0)),
            pl.BlockSpec((N_MOD, SUBLANES, D_MODEL), lambda i, j: (0, 0, 0)),
            pl.BlockSpec((1, D_MODEL), lambda i, j: (0, 0)),
            pl.BlockSpec((D_MODEL, tn), lambda i, j: (0, j)),
        ],
        out_specs=[pl.BlockSpec((tm, q), lambda i, j: (i, j))] * 3,
        out_shape=[out, out, out],
        scratch_shapes=[pltpu.VMEM((tm, D_MODEL), BF16)],
        compiler_params=_params("parallel", "arbitrary"),
        name="inproj",
    )(x, mod, g_mix, w_in_tiled)


def _s5_kernel(*refs, batch, has_init):
    if has_init:
        u_ref, bm_ref, cm_ref, a_ref, d_ref, s0_ref, y_ref, fin_ref, wf, wb, xs = refs
    else:
        u_ref, bm_ref, cm_ref, a_ref, d_ref, y_ref, fin_ref, wf, wb, xs = refs
        s0_ref = None
    rows = u_ref.shape[0]
    rc = SCAN_STEPS * batch
    n_chunks = rows // rc
    halves = batch // SUBLANES

    y_ref[...] = d_ref[...] * u_ref[...]
    if has_init:
        xs[...] = s0_ref[0]
    else:
        xs[...] = jnp.zeros_like(xs)

    def chunk(ci, carry):
        rf = pl.multiple_of(ci * rc, rc)
        rb = pl.multiple_of((n_chunks - 1 - ci) * rc, rc)
        uf = u_ref[pl.ds(rf, rc), :].astype(BF16)
        ub = u_ref[pl.ds(rb, rc), :].astype(BF16)
        wf[...] = jnp.dot(uf, bm_ref[0, :, :STATE_LANES], preferred_element_type=F32)
        wb[...] = jnp.dot(ub, bm_ref[0, :, STATE_LANES:], preferred_element_type=F32)
        for half in range(halves):
            hs = slice(half * SUBLANES, (half + 1) * SUBLANES)
            for pair in range(PAIRS):
                re = slice(pair * 2 * LANES, pair * 2 * LANES + LANES)
                im = slice(pair * 2 * LANES + LANES, (pair + 1) * 2 * LANES)
                for d, w in ((0, wf), (1, wb)):
                    al = slice(d * PAIRS * LANES + pair * LANES, d * PAIRS * LANES + (pair + 1) * LANES)
                    ar = a_ref[0, 0, :, al]
                    ai = a_ref[0, 1, :, al]
                    xr = xs[d, hs, re]
                    xi = xs[d, hs, im]
                    for t in range(SCAN_STEPS):
                        step = t if d == 0 else SCAN_STEPS - 1 - t
                        rs = slice(step * batch + half * SUBLANES, step * batch + (half + 1) * SUBLANES)
                        nr = ar * xr - ai * xi + w[rs, re]
                        ni = ar * xi + ai * xr + w[rs, im]
                        w[rs, re] = nr
                        w[rs, im] = ni
                        xr, xi = nr, ni
                    xs[d, hs, re] = xr
                    xs[d, hs, im] = xi
        y_ref[pl.ds(rf, rc), :] += jnp.dot(wf[...].astype(BF16), cm_ref[0, :STATE_LANES, :],
                                           preferred_element_type=F32)
        y_ref[pl.ds(rb, rc), :] += jnp.dot(wb[...].astype(BF16), cm_ref[0, STATE_LANES:, :],
                                           preferred_element_type=F32)
        return carry

    lax.fori_loop(0, n_chunks, chunk, 0)
    fin_ref[0] = xs[...]


def _s5(u, bm, cm, a, d_skip, s0, batch):
    r = u.shape[0]
    rc = SCAN_STEPS * batch
    has_init = s0 is not None
    in_specs = [
        pl.BlockSpec((r, LANES), lambda g: (0, g)),
        pl.BlockSpec((1, LANES, N_DIR * STATE_LANES), lambda g: (g, 0, 0)),
        pl.BlockSpec((1, N_DIR * STATE_LANES, LANES), lambda g: (g, 0, 0)),
        pl.BlockSpec((1, 2, SUBLANES, N_DIR * PAIRS * LANES), lambda g: (g, 0, 0, 0)),
        pl.BlockSpec((1, LANES), lambda g: (0, g)),
    ]
    args = [u, bm, cm, a, d_skip]
    if has_init:
        in_specs.append(pl.BlockSpec((1, N_DIR, batch, STATE_LANES), lambda g: (g, 0, 0, 0)))
        args.append(s0)
    return pl.pallas_call(
        functools.partial(_s5_kernel, batch=batch, has_init=has_init),
        grid=(N_GROUP_BLOCKS,),
        in_specs=in_specs,
        out_specs=[
            pl.BlockSpec((r, LANES), lambda g: (0, g)),
            pl.BlockSpec((1, N_DIR, batch, STATE_LANES), lambda g: (g, 0, 0, 0)),
        ],
        out_shape=[
            jax.ShapeDtypeStruct((r, D_SSM), F32),
            jax.ShapeDtypeStruct((N_GROUP_BLOCKS, N_DIR, batch, STATE_LANES), F32),
        ],
        scratch_shapes=[
            pltpu.VMEM((rc, STATE_LANES), F32),
            pltpu.VMEM((rc, STATE_LANES), F32),
            pltpu.VMEM((N_DIR, batch, STATE_LANES), F32),
        ],
        compiler_params=_params("parallel"),
        name="s5_mixer",
    )(*args)


def _mixout_kernel(y_ref, gb_ref, zc_ref, zp_ref, zn_ref, x_ref, mod_ref, wglu_ref, wout_ref,
                   cw_ref, cb_ref, g_ref, x1_ref, h2_ref, *, batch, seg):
    tm = x_ref.shape[0]
    tl = tm // batch
    l0 = pl.program_id(0) * tl
    prev_ok = (l0 % seg != 0).astype(F32)
    next_ok = ((l0 + tl) % seg != 0).astype(F32)

    y = jax.nn.gelu(y_ref[...])
    y_ssm = y * jax.nn.sigmoid(jnp.dot(y.astype(BF16), wglu_ref[...], preferred_element_type=F32))

    zc = zc_ref[...]
    prev = jnp.concatenate([zp_ref[...] * prev_ok, zc[:tm - batch]], axis=0)
    nxt = jnp.concatenate([zc[batch:], zn_ref[...] * next_ok], axis=0)
    conv = prev * cw_ref[0:1, :] + zc * cw_ref[1:2, :] + nxt * cw_ref[2:3, :] + cb_ref[...]
    y_conv = gb_ref[...] * conv

    out = jnp.dot(y_ssm.astype(BF16), wout_ref[:D_SSM, :], preferred_element_type=F32)
    out += jnp.dot(y_conv.astype(BF16), wout_ref[D_SSM:, :], preferred_element_type=F32)
    x1 = (_rows8(x_ref[...]) + mod_ref[2][None] * _rows8(out)).reshape(tm, D_MODEL)
    x1_ref[...] = x1
    h2_ref[...] = _norm_mod(x1, g_ref[...], mod_ref[4], mod_ref[3]).astype(BF16)


def _mixout(y, gb, zc, x, mod, w_glu, w_out, conv_w, conv_b, g_ffn, batch, seg, tm):
    r = x.shape[0]
    hb = tm // batch
    n_hb = r // batch
    row = lambda i: (i, 0)
    full = lambda i: (0, 0)
    return pl.pallas_call(
        functools.partial(_mixout_kernel, batch=batch, seg=seg),
        grid=(r // tm,),
        in_specs=[
            pl.BlockSpec((tm, D_SSM), row),
            pl.BlockSpec((tm, D_CONV), row),
            pl.BlockSpec((tm, D_CONV), row),
            pl.BlockSpec((batch, D_CONV), lambda i: (jnp.maximum(i * hb - 1, 0), 0)),
            pl.BlockSpec((batch, D_CONV), lambda i: (jnp.minimum((i + 1) * hb, n_hb - 1), 0)),
            pl.BlockSpec((tm, D_MODEL), row),
            pl.BlockSpec((N_MOD, SUBLANES, D_MODEL), lambda i: (0, 0, 0)),
            pl.BlockSpec((D_SSM, D_SSM), full),
            pl.BlockSpec((D_MODEL, D_MODEL), full),
            pl.BlockSpec((3, D_CONV), full),
            pl.BlockSpec((1, D_CONV), full),
            pl.BlockSpec((1, D_MODEL), full),
        ],
        out_specs=[pl.BlockSpec((tm, D_MODEL), row), pl.BlockSpec((tm, D_MODEL), row)],
        out_shape=[jax.ShapeDtypeStruct((r, D_MODEL), F32), jax.ShapeDtypeStruct((r, D_MODEL), BF16)],
        compiler_params=_params("parallel"),
        name="mixer_out",
    )(y, gb, zc, zc, zc, x, mod, w_glu, w_out, conv_w, conv_b, g_ffn)


def _ffn_kernel(h_ref, x_ref, mod_ref, wg_ref, wu_ref, wd_ref, gf_ref, o_ref, acc, *, final_norm):
    k = pl.program_id(1)

    @pl.when(k == 0)
    def _():
        acc[...] = jnp.zeros_like(acc)

    h = h_ref[...]
    gate = jnp.dot(h, wg_ref[...], preferred_element_type=F32)
    up = jnp.dot(h, wu_ref[...], preferred_element_type=F32)
    act = (jax.nn.silu(gate) * up).astype(BF16)
    acc[...] += jnp.dot(act, wd_ref[...], preferred_element_type=F32)

    @pl.when(k == pl.num_programs(1) - 1)
    def _():
        x2 = (_rows8(x_ref[...]) + mod_ref[5][None] * _rows8(acc[...])).reshape(x_ref.shape)
        if final_norm:
            ms = jnp.mean(x2 * x2, axis=-1, keepdims=True)
            x2 = x2 * lax.rsqrt(ms + EPS) * gf_ref[...]
        o_ref[...] = x2


def _ffn(h2, x1, mod, w_gate, w_up, w_down, g_final, final_norm, tm):
    r = x1.shape[0]
    tf = 512
    return pl.pallas_call(
        functools.partial(_ffn_kernel, final_norm=final_norm),
        grid=(r // tm, D_FF // tf),
        in_specs=[
            pl.BlockSpec((tm, D_MODEL), lambda i, k: (i, 0)),
            pl.BlockSpec((tm, D_MODEL), lambda i, k: (i, 0)),
            pl.BlockSpec((N_MOD, SUBLANES, D_MODEL), lambda i, k: (0, 0, 0)),
            pl.BlockSpec((D_MODEL, tf), lambda i, k: (0, k)),
            pl.BlockSpec((D_MODEL, tf), lambda i, k: (0, k)),
            pl.BlockSpec((tf, D_MODEL), lambda i, k: (k, 0)),
            pl.BlockSpec((1, D_MODEL), lambda i, k: (0, 0)),
        ],
        out_specs=pl.BlockSpec((tm, D_MODEL), lambda i, k: (i, 0)),
        out_shape=jax.ShapeDtypeStruct((r, D_MODEL), F32),
        scratch_shapes=[pltpu.VMEM((tm, D_MODEL), F32)],
        compiler_params=_params("parallel", "arbitrary"),
        name="ffn",
    )(h2, x1, mod, w_gate, w_up, w_down, g_final)


def _s5_matrices(lam_re, lam_im, log_dt, b_re, b_im, c_re, c_im):
    lr = jnp.minimum(lam_re, LAM_RE_MAX)
    li = lam_im
    dt = jnp.exp(log_dt)[..., None]
    mag = jnp.exp(lr * dt)
    ar = mag * jnp.cos(li * dt)
    ai = mag * jnp.sin(li * dt)
    den = lr * lr + li * li
    qr = ((ar - 1.0) * lr + ai * li) / den
    qi = (ai * lr - (ar - 1.0) * li) / den
    bbr = qr[..., None] * b_re - qi[..., None] * b_im
    bbi = qr[..., None] * b_im + qi[..., None] * b_re

    nb, gp = N_GROUP_BLOCKS, GROUPS_PER_BLOCK
    sel = jnp.eye(gp, dtype=F32).reshape(gp, PAIRS, 2)
    tb = jnp.stack([bbr, bbi], axis=2)
    tb = tb.reshape(N_DIR, nb, gp, 2, SSM_STATE, SSM_H)
    bm = jnp.einsum("dbgrph,gqs->bghdqrsp", tb, sel)
    bm = bm.reshape(nb, gp * SSM_H, N_DIR * STATE_LANES)
    tc = jnp.stack([c_re, -c_im], axis=2)
    tc = tc.reshape(N_DIR, nb, gp, 2, SSM_H, SSM_STATE)
    cm = jnp.einsum("dbgrhp,gqs->bdqrspgh", tc, sel)
    cm = cm.reshape(nb, N_DIR * STATE_LANES, gp * SSM_H)
    a = jnp.stack([ar, ai], axis=0)
    a = a.reshape(2, N_DIR, nb, PAIRS * LANES).transpose(2, 0, 1, 3).reshape(nb, 2, 1, N_DIR * PAIRS * LANES)
    a = jnp.broadcast_to(a, (nb, 2, SUBLANES, N_DIR * PAIRS * LANES))
    return bm.astype(BF16), cm.astype(BF16), a


def _tile_w_in(w):
    q = 256
    w4 = w.reshape(D_MODEL, 4, D_SSM // q, q)
    return w4.transpose(0, 2, 1, 3).reshape(D_MODEL, 4 * D_SSM)


def _init_states(st):
    b = st.shape[0]
    s = st.reshape(b, N_DIR, 2, N_GROUP_BLOCKS, PAIRS, 2, SSM_STATE)
    return s.transpose(3, 1, 0, 4, 2, 5, 6).reshape(N_GROUP_BLOCKS, N_DIR, b, STATE_LANES)


def _final_states(fin):
    b = fin.shape[2]
    s = fin.reshape(N_GROUP_BLOCKS, N_DIR, b, PAIRS, 2, 2, SSM_STATE)
    return s.transpose(2, 1, 4, 0, 3, 5, 6).reshape(b, N_DIR, 2, SSM_GROUPS, SSM_STATE)


def _time_major(x):
    b, l, d = x.shape
    return x.transpose(1, 0, 2).reshape(l * b, d)


def _batch_major(x, b):
    r, d = x.shape
    return x.reshape(r // b, b, d).transpose(1, 0, 2)


def kernel(x_prompt, x_sample, state_ssm, c, c_ctx, w_ada, b_ada, g_mix, w_in, ssm_lam_re, ssm_lam_im, ssm_log_dt, ssm_b_re, ssm_b_im, ssm_c_re, ssm_c_im, ssm_d, w_glu, conv_w, conv_b, w_out, g_ffn, w_gate, w_up, w_down, g_final):
    n_ctx, l_ctx, _ = x_prompt.shape
    n_dec, l_dec, _ = x_sample.shape

    cond = jnp.zeros((16, D_MODEL), F32).at[0].set(c_ctx).at[1:1 + n_dec].set(c)
    mod = _modulation(cond, w_ada, b_ada).reshape(DEPTH, 16, N_MOD, D_MODEL)
    mod_ctx = jnp.broadcast_to(mod[:, 0, :, None, :], (DEPTH, N_MOD, SUBLANES, D_MODEL))
    mod_dec = mod[:, 1:1 + n_dec].transpose(0, 2, 1, 3)

    groups = [
        dict(x=_time_major(x_prompt), batch=n_ctx, seg=l_ctx, mod=mod_ctx, s0=None),
        dict(x=_time_major(x_sample), batch=n_dec, seg=GRID_W, mod=mod_dec, s0=state_ssm.astype(F32)),
    ]

    finals = []
    for l in range(DEPTH):
        bm, cm, a = _s5_matrices(ssm_lam_re[l], ssm_lam_im[l], ssm_log_dt[l], ssm_b_re[l], ssm_b_im[l],
                                 ssm_c_re[l], ssm_c_im[l])
        w_in_t = _tile_w_in(w_in[l]).astype(BF16)
        w_glu_l = w_glu[l].astype(BF16)
        w_out_l = w_out[l].astype(BF16)
        w_gate_l = w_gate[l].astype(BF16)
        w_up_l = w_up[l].astype(BF16)
        w_down_l = w_down[l].astype(BF16)
        for grp in groups:
            batch = grp["batch"]
            mod_l = grp["mod"][l]
            s0 = None if grp["s0"] is None else _init_states(grp["s0"][:, l])
            u, gb, zc = _inproj(grp["x"], mod_l, g_mix[l][None], w_in_t, tm=1024)
            y, fin = _s5(u, bm, cm, a, ssm_d[l][None], s0, batch)
            x1, h2 = _mixout(y, gb, zc, grp["x"], mod_l, w_glu_l, w_out_l, conv_w[l], conv_b[l][None],
                             g_ffn[l][None], batch, grp["seg"], tm=256)
            grp["x"] = _ffn(h2, x1, mod_l, w_gate_l, w_up_l, w_down_l, g_final[None],
                            final_norm=(l == DEPTH - 1), tm=512)
            if grp["s0"] is None:
                finals.append(_final_states(fin))

    y_prompt = _batch_major(groups[0]["x"], n_ctx)
    y_sample = _batch_major(groups[1]["x"], n_dec)
    new_state = jnp.stack(finals, axis=1).astype(x_prompt.dtype)
    return (y_prompt, y_sample, new_state)
```

```python
import functools

import jax
import jax.numpy as jnp
from jax import lax
from jax.experimental import pallas as pl
from jax.experimental.pallas import tpu as pltpu

D_MODEL = 2048
DEPTH = 2
GRID_W = 64
D_SSM = 1024
D_CONV = 1024
SSM_H = 16
SSM_GROUPS = 64
SSM_STATE = 64
N_DIR = 2
D_FF = 5632
N_MOD = 6
EPS = 1e-6
LAM_RE_MAX = -1e-4

SUBLANES = 8
LANES = 128
GROUPS_PER_BLOCK = LANES // SSM_H
N_GROUP_BLOCKS = SSM_GROUPS // GROUPS_PER_BLOCK
PAIRS = GROUPS_PER_BLOCK // 2
STATE_LANES = GROUPS_PER_BLOCK * 2 * SSM_STATE
SCAN_STEPS = 64
VMEM_LIMIT = 56 * 1024 * 1024

TM_INPROJ = 1024
TM_MIXOUT = 512
TM_FFN = 512
TN_INPROJ = 1024
TF_FFN = 512

F32 = jnp.float32
BF16 = jnp.bfloat16


def _params(*sem):
    return pltpu.CompilerParams(dimension_semantics=sem, vmem_limit_bytes=VMEM_LIMIT)


def _resident(shape, index_map):
    return pl.BlockSpec(shape, index_map, pipeline_mode=pl.Buffered(1))


def _rows8(a):
    return a.reshape(a.shape[0] // SUBLANES, SUBLANES, a.shape[1])


def _norm_mod(x, gain, scale, shift):
    ms = jnp.mean(x * x, axis=-1, keepdims=True)
    y = x * lax.rsqrt(ms + EPS) * gain
    h = _rows8(y) * (1.0 + scale)[None] + shift[None]
    return h.reshape(x.shape)


def _mod_kernel(cond_ref, w_ref, b_ref, o_ref):
    s = jax.nn.silu(cond_ref[...]).astype(BF16)
    o_ref[0] = jnp.dot(s, w_ref[0].astype(BF16), preferred_element_type=F32) + b_ref[0]


def _modulation(cond, w_ada, b_ada):
    tn = 1024
    n = N_MOD * D_MODEL
    return pl.pallas_call(
        _mod_kernel,
        grid=(DEPTH, n // tn),
        in_specs=[
            pl.BlockSpec((16, D_MODEL), lambda l, j: (0, 0)),
            pl.BlockSpec((1, D_MODEL, tn), lambda l, j: (l, 0, j)),
            pl.BlockSpec((1, 1, tn), lambda l, j: (l, 0, j)),
        ],
        out_specs=pl.BlockSpec((1, 16, tn), lambda l, j: (l, 0, j)),
        out_shape=jax.ShapeDtypeStruct((DEPTH, 16, n), F32),
        compiler_params=_params("parallel", "parallel"),
        name="adaln_modulation",
    )(cond, w_ada, b_ada.reshape(DEPTH, 1, n))


def _inproj_kernel(x_ref, mod_ref, g_ref, w_ref, *out_refs, batch_major):
    if batch_major:
        xt_ref, u_ref, gb_ref, zc_ref, h_scr = out_refs
    else:
        u_ref, gb_ref, zc_ref, h_scr = out_refs

    @pl.when(pl.program_id(1) == 0)
    def _():
        if batch_major:
            x = jnp.concatenate([x_ref[:, t, :] for t in range(x_ref.shape[1])], axis=0)
            xt_ref[...] = x
        else:
            x = x_ref[...]
        h = _norm_mod(x, g_ref[0], mod_ref[0, 1], mod_ref[0, 0])
        h_scr[...] = h.astype(BF16)

    z = jnp.dot(h_scr[...], w_ref[0], preferred_element_type=F32)
    q = z.shape[1] // 4
    u_ref[...] = z[:, :q]
    gb_ref[...] = z[:, q:2 * q]
    zc_ref[...] = z[:, 2 * q:3 * q] * z[:, 3 * q:]


def _inproj(x, mod, g_mix, w_in_tiled, layer, batch_major):
    tn = TN_INPROJ
    tm = TM_INPROJ // 2 if batch_major else TM_INPROJ
    q = tn // 4
    if batch_major:
        b, l, _ = x.shape
        r = b * l
        x_spec = pl.BlockSpec((b, tm // b, D_MODEL), lambda i, j: (0, i, 0))
    else:
        r = x.shape[0]
        x_spec = pl.BlockSpec((tm, D_MODEL), lambda i, j: (i, 0))
    out = jax.ShapeDtypeStruct((r, D_SSM), F32)
    out_specs = [pl.BlockSpec((tm, q), lambda i, j: (i, j))] * 3
    out_shape = [out, out, out]
    if batch_major:
        out_specs = [pl.BlockSpec((tm, D_MODEL), lambda i, j: (i, 0))] + out_specs
        out_shape = [jax.ShapeDtypeStruct((r, D_MODEL), F32)] + out_shape
    return pl.pallas_call(
        functools.partial(_inproj_kernel, batch_major=batch_major),
        grid=(r // tm, 4 * D_SSM // tn),
        in_specs=[
            x_spec,
            pl.BlockSpec((1, N_MOD, SUBLANES, D_MODEL), lambda i, j: (layer, 0, 0, 0)),
            pl.BlockSpec((1, 1, D_MODEL), lambda i, j: (layer, 0, 0)),
            pl.BlockSpec((1, D_MODEL, tn), lambda i, j: (layer, 0, j)),
        ],
        out_specs=out_specs,
        out_shape=out_shape,
        scratch_shapes=[pltpu.VMEM((tm, D_MODEL), BF16)],
        compiler_params=_params("parallel", "arbitrary"),
        name="inproj",
    )(x, mod, g_mix, w_in_tiled)


def _s5_kernel(*refs, batch, has_init):
    if has_init:
        u_ref, bm_ref, cm_ref, a_ref, d_ref, s0_ref, y_ref, fin_ref, wf, wb, xs = refs
    else:
        u_ref, bm_ref, cm_ref, a_ref, d_ref, y_ref, fin_ref, wf, wb, xs = refs
        s0_ref = None
    rows = u_ref.shape[0]
    rc = SCAN_STEPS * batch
    n_chunks = rows // rc
    halves = batch // SUBLANES

    y_ref[...] = d_ref[0] * u_ref[...]
    if has_init:
        xs[...] = s0_ref[0]
    else:
        xs[...] = jnp.zeros_like(xs)

    def chunk(ci, carry):
        rf = pl.multiple_of(ci * rc, rc)
        rb = pl.multiple_of((n_chunks - 1 - ci) * rc, rc)
        uf = u_ref[pl.ds(rf, rc), :].astype(BF16)
        ub = u_ref[pl.ds(rb, rc), :].astype(BF16)
        wf[...] = jnp.dot(uf, bm_ref[0, :, :STATE_LANES], preferred_element_type=F32)
        wb[...] = jnp.dot(ub, bm_ref[0, :, STATE_LANES:], preferred_element_type=F32)
        for half in range(halves):
            hs = slice(half * SUBLANES, (half + 1) * SUBLANES)
            for pair in range(PAIRS):
                re = slice(pair * 2 * LANES, pair * 2 * LANES + LANES)
                im = slice(pair * 2 * LANES + LANES, (pair + 1) * 2 * LANES)
                for d, w in ((0, wf), (1, wb)):
                    al = slice(d * PAIRS * LANES + pair * LANES, d * PAIRS * LANES + (pair + 1) * LANES)
                    ar = a_ref[0, 0, :, al]
                    ai = a_ref[0, 1, :, al]
                    xr = xs[d, hs, re]
                    xi = xs[d, hs, im]
                    for t in range(SCAN_STEPS):
                        step = t if d == 0 else SCAN_STEPS - 1 - t
                        rs = slice(step * batch + half * SUBLANES, step * batch + (half + 1) * SUBLANES)
                        nr = ar * xr - ai * xi + w[rs, re]
                        ni = ar * xi + ai * xr + w[rs, im]
                        w[rs, re] = nr
                        w[rs, im] = ni
                        xr, xi = nr, ni
                    xs[d, hs, re] = xr
                    xs[d, hs, im] = xi
        y_ref[pl.ds(rf, rc), :] += jnp.dot(wf[...].astype(BF16), cm_ref[0, :STATE_LANES, :],
                                           preferred_element_type=F32)
        y_ref[pl.ds(rb, rc), :] += jnp.dot(wb[...].astype(BF16), cm_ref[0, STATE_LANES:, :],
                                           preferred_element_type=F32)
        return carry

    lax.fori_loop(0, n_chunks, chunk, 0)
    fin_ref[0] = xs[...]


def _s5(u, bm, cm, a, d_skip, s0, batch, layer):
    r = u.shape[0]
    rc = SCAN_STEPS * batch
    has_init = s0 is not None
    in_specs = [
        pl.BlockSpec((r, LANES), lambda g: (0, g)),
        pl.BlockSpec((1, LANES, N_DIR * STATE_LANES), lambda g: (g, 0, 0)),
        pl.BlockSpec((1, N_DIR * STATE_LANES, LANES), lambda g: (g, 0, 0)),
        pl.BlockSpec((1, 2, SUBLANES, N_DIR * PAIRS * LANES), lambda g: (g, 0, 0, 0)),
        pl.BlockSpec((1, 1, LANES), lambda g: (layer, 0, g)),
    ]
    args = [u, bm, cm, a, d_skip]
    if has_init:
        in_specs.append(pl.BlockSpec((1, N_DIR, batch, STATE_LANES), lambda g: (g, 0, 0, 0)))
        args.append(s0)
    return pl.pallas_call(
        functools.partial(_s5_kernel, batch=batch, has_init=has_init),
        grid=(N_GROUP_BLOCKS,),
        in_specs=in_specs,
        out_specs=[
            pl.BlockSpec((r, LANES), lambda g: (0, g)),
            pl.BlockSpec((1, N_DIR, batch, STATE_LANES), lambda g: (g, 0, 0, 0)),
        ],
        out_shape=[
            jax.ShapeDtypeStruct((r, D_SSM), F32),
            jax.ShapeDtypeStruct((N_GROUP_BLOCKS, N_DIR, batch, STATE_LANES), F32),
        ],
        scratch_shapes=[
            pltpu.VMEM((rc, STATE_LANES), F32),
            pltpu.VMEM((rc, STATE_LANES), F32),
            pltpu.VMEM((N_DIR, batch, STATE_LANES), F32),
        ],
        compiler_params=_params("parallel"),
        name="s5_mixer",
    )(*args)


def _mixout_kernel(y_ref, gb_ref, zc_ref, zp_ref, zn_ref, x_ref, mod_ref, wglu_ref, wout_ref,
                   cw_ref, cb_ref, g_ref, x1_ref, h2_ref, *, batch, seg):
    tm = x_ref.shape[0]
    tl = tm // batch
    l0 = pl.program_id(0) * tl
    prev_ok = (l0 % seg != 0).astype(F32)
    next_ok = ((l0 + tl) % seg != 0).astype(F32)

    y = jax.nn.gelu(y_ref[...])
    y_ssm = y * jax.nn.sigmoid(jnp.dot(y.astype(BF16), wglu_ref[0], preferred_element_type=F32))

    zc = zc_ref[...]
    prev = jnp.concatenate([zp_ref[...] * prev_ok, zc[:tm - batch]], axis=0)
    nxt = jnp.concatenate([zc[batch:], zn_ref[...] * next_ok], axis=0)
    conv = prev * cw_ref[0, 0:1, :] + zc * cw_ref[0, 1:2, :] + nxt * cw_ref[0, 2:3, :] + cb_ref[0]
    y_conv = gb_ref[...] * conv

    out = jnp.dot(y_ssm.astype(BF16), wout_ref[0, :D_SSM, :], preferred_element_type=F32)
    out += jnp.dot(y_conv.astype(BF16), wout_ref[0, D_SSM:, :], preferred_element_type=F32)
    x1 = (_rows8(x_ref[...]) + mod_ref[0, 2][None] * _rows8(out)).reshape(tm, D_MODEL)
    x1_ref[...] = x1
    h2_ref[...] = _norm_mod(x1, g_ref[0], mod_ref[0, 4], mod_ref[0, 3]).astype(BF16)


def _mixout(y, gb, zc, x, mod, w_glu, w_out, conv_w, conv_b, g_ffn, batch, seg, layer):
    r = x.shape[0]
    tm = TM_MIXOUT
    assert seg % (tm // batch) == 0
    hb = tm // batch
    n_hb = r // batch
    row = lambda i: (i, 0)
    lay = lambda i: (layer, 0, 0)
    return pl.pallas_call(
        functools.partial(_mixout_kernel, batch=batch, seg=seg),
        grid=(r // tm,),
        in_specs=[
            pl.BlockSpec((tm, D_SSM), row),
            pl.BlockSpec((tm, D_CONV), row),
            pl.BlockSpec((tm, D_CONV), row),
            pl.BlockSpec((batch, D_CONV), lambda i: (jnp.maximum(i * hb - 1, 0), 0)),
            pl.BlockSpec((batch, D_CONV), lambda i: (jnp.minimum((i + 1) * hb, n_hb - 1), 0)),
            pl.BlockSpec((tm, D_MODEL), row),
            _resident((1, N_MOD, SUBLANES, D_MODEL), lambda i: (layer, 0, 0, 0)),
            _resident((1, D_SSM, D_SSM), lay),
            _resident((1, D_MODEL, D_MODEL), lay),
            _resident((1, 3, D_CONV), lay),
            _resident((1, 1, D_CONV), lay),
            _resident((1, 1, D_MODEL), lay),
        ],
        out_specs=[pl.BlockSpec((tm, D_MODEL), row), pl.BlockSpec((tm, D_MODEL), row)],
        out_shape=[jax.ShapeDtypeStruct((r, D_MODEL), F32), jax.ShapeDtypeStruct((r, D_MODEL), BF16)],
        compiler_params=_params("parallel"),
        name="mixer_out",
    )(y, gb, zc, zc, zc, x, mod, w_glu, w_out, conv_w, conv_b, g_ffn)


def _ffn_kernel(h_ref, x_ref, mod_ref, wg_ref, wu_ref, wd_ref, gf_ref, o_ref, acc, *, final):
    k = pl.program_id(1)

    @pl.when(k == 0)
    def _():
        acc[...] = jnp.zeros_like(acc)

    h = h_ref[...]
    gate = jnp.dot(h, wg_ref[0], preferred_element_type=F32)
    up = jnp.dot(h, wu_ref[0], preferred_element_type=F32)
    act = (jax.nn.silu(gate) * up).astype(BF16)
    acc[...] += jnp.dot(act, wd_ref[0], preferred_element_type=F32)

    @pl.when(k == pl.num_programs(1) - 1)
    def _():
        x2 = (_rows8(x_ref[...]) + mod_ref[0, 5][None] * _rows8(acc[...])).reshape(x_ref.shape)
        if final:
            ms = jnp.mean(x2 * x2, axis=-1, keepdims=True)
            x2 = x2 * lax.rsqrt(ms + EPS) * gf_ref[...]
            b = o_ref.shape[0]
            for t in range(o_ref.shape[1]):
                o_ref[:, t, :] = x2[t * b:(t + 1) * b, :]
        else:
            o_ref[...] = x2


def _ffn(h2, x1, mod, w_gate, w_up, w_down, g_final, layer, batch, final):
    r = x1.shape[0]
    tm, tf = TM_FFN, TF_FFN
    if final:
        out_spec = pl.BlockSpec((batch, tm // batch, D_MODEL), lambda i, k: (0, i, 0))
        out_shape = jax.ShapeDtypeStruct((batch, r // batch, D_MODEL), F32)
    else:
        out_spec = pl.BlockSpec((tm, D_MODEL), lambda i, k: (i, 0))
        out_shape = jax.ShapeDtypeStruct((r, D_MODEL), F32)
    return pl.pallas_call(
        functools.partial(_ffn_kernel, final=final),
        grid=(r // tm, D_FF // tf),
        in_specs=[
            pl.BlockSpec((tm, D_MODEL), lambda i, k: (i, 0)),
            pl.BlockSpec((tm, D_MODEL), lambda i, k: (i, 0)),
            pl.BlockSpec((1, N_MOD, SUBLANES, D_MODEL), lambda i, k: (layer, 0, 0, 0)),
            pl.BlockSpec((1, D_MODEL, tf), lambda i, k: (layer, 0, k)),
            pl.BlockSpec((1, D_MODEL, tf), lambda i, k: (layer, 0, k)),
            pl.BlockSpec((1, tf, D_MODEL), lambda i, k: (layer, k, 0)),
            pl.BlockSpec((1, D_MODEL), lambda i, k: (0, 0)),
        ],
        out_specs=out_spec,
        out_shape=out_shape,
        scratch_shapes=[pltpu.VMEM((tm, D_MODEL), F32)],
        compiler_params=_params("parallel", "arbitrary"),
        name="ffn",
    )(h2, x1, mod, w_gate, w_up, w_down, g_final)


def _s5_matrices(lam_re, lam_im, log_dt, b_re, b_im, c_re, c_im):
    lr = jnp.minimum(lam_re, LAM_RE_MAX)
    li = lam_im
    dt = jnp.exp(log_dt)[..., None]
    mag = jnp.exp(lr * dt)
    ar = mag * jnp.cos(li * dt)
    ai = mag * jnp.sin(li * dt)
    den = lr * lr + li * li
    qr = ((ar - 1.0) * lr + ai * li) / den
    qi = (ai * lr - (ar - 1.0) * li) / den
    bbr = qr[..., None] * b_re - qi[..., None] * b_im
    bbi = qr[..., None] * b_im + qi[..., None] * b_re

    nb, gp = N_GROUP_BLOCKS, GROUPS_PER_BLOCK
    sel = jnp.eye(gp, dtype=F32).reshape(gp, PAIRS, 2)
    tb = jnp.stack([bbr, bbi], axis=2)
    tb = tb.reshape(N_DIR, nb, gp, 2, SSM_STATE, SSM_H)
    bm = jnp.einsum("dbgrph,gqs->bghdqrsp", tb, sel)
    bm = bm.reshape(nb, gp * SSM_H, N_DIR * STATE_LANES)
    tc = jnp.stack([c_re, -c_im], axis=2)
    tc = tc.reshape(N_DIR, nb, gp, 2, SSM_H, SSM_STATE)
    cm = jnp.einsum("dbgrhp,gqs->bdqrspgh", tc, sel)
    cm = cm.reshape(nb, N_DIR * STATE_LANES, gp * SSM_H)
    a = jnp.stack([ar, ai], axis=0)
    a = a.reshape(2, N_DIR, nb, PAIRS * LANES).transpose(2, 0, 1, 3).reshape(nb, 2, 1, N_DIR * PAIRS * LANES)
    a = jnp.broadcast_to(a, (nb, 2, SUBLANES, N_DIR * PAIRS * LANES))
    return bm.astype(BF16), cm.astype(BF16), a


def _tile_w_in(w):
    q = TN_INPROJ // 4
    w4 = w.reshape(DEPTH, D_MODEL, 4, D_SSM // q, q)
    return w4.transpose(0, 1, 3, 2, 4).reshape(DEPTH, D_MODEL, 4 * D_SSM)


def _init_states(st):
    b = st.shape[0]
    s = st.reshape(b, N_DIR, 2, N_GROUP_BLOCKS, PAIRS, 2, SSM_STATE)
    return s.transpose(3, 1, 0, 4, 2, 5, 6).reshape(N_GROUP_BLOCKS, N_DIR, b, STATE_LANES)


def _final_states(fin):
    b = fin.shape[2]
    s = fin.reshape(N_GROUP_BLOCKS, N_DIR, b, PAIRS, 2, 2, SSM_STATE)
    return s.transpose(2, 1, 4, 0, 3, 5, 6).reshape(b, N_DIR, 2, SSM_GROUPS, SSM_STATE)


def kernel(x_prompt, x_sample, state_ssm, c, c_ctx, w_ada, b_ada, g_mix, w_in, ssm_lam_re, ssm_lam_im, ssm_log_dt, ssm_b_re, ssm_b_im, ssm_c_re, ssm_c_im, ssm_d, w_glu, conv_w, conv_b, w_out, g_ffn, w_gate, w_up, w_down, g_final):
    n_ctx, l_ctx, _ = x_prompt.shape
    n_dec, l_dec, _ = x_sample.shape

    cond = jnp.zeros((16, D_MODEL), F32).at[0].set(c_ctx).at[1:1 + n_dec].set(c)
    mod = _modulation(cond, w_ada, b_ada).reshape(DEPTH, 16, N_MOD, D_MODEL)
    mod_ctx = jnp.broadcast_to(mod[:, 0, :, None, :], (DEPTH, N_MOD, SUBLANES, D_MODEL))
    mod_dec = mod[:, 1:1 + n_dec].transpose(0, 2, 1, 3)

    w_in_t = _tile_w_in(w_in).astype(BF16)
    w_glu_b = w_glu.astype(BF16)
    w_out_b = w_out.astype(BF16)
    w_gate_b = w_gate.astype(BF16)
    w_up_b = w_up.astype(BF16)
    w_down_b = w_down.astype(BF16)
    g_mix3 = g_mix[:, None, :]
    g_ffn3 = g_ffn[:, None, :]
    conv_b3 = conv_b[:, None, :]
    ssm_d3 = ssm_d[:, None, :]

    groups = [
        dict(x=x_prompt, batch=n_ctx, seg=l_ctx, mod=mod_ctx, s0=None),
        dict(x=x_sample, batch=n_dec, seg=GRID_W, mod=mod_dec, s0=state_ssm.astype(F32)),
    ]

    finals = []
    for l in range(DEPTH):
        bm, cm, a = _s5_matrices(ssm_lam_re[l], ssm_lam_im[l], ssm_log_dt[l], ssm_b_re[l], ssm_b_im[l],
                                 ssm_c_re[l], ssm_c_im[l])
        for grp in groups:
            batch = grp["batch"]
            s0 = None if grp["s0"] is None else _init_states(grp["s0"][:, l])
            if l == 0:
                x, u, gb, zc = _inproj(grp["x"], grp["mod"], g_mix3, w_in_t, l, batch_major=True)
            else:
                x = grp["x"]
                u, gb, zc = _inproj(x, grp["mod"], g_mix3, w_in_t, l, batch_major=False)
            y, fin = _s5(u, bm, cm, a, ssm_d3, s0, batch, l)
            x1, h2 = _mixout(y, gb, zc, x, grp["mod"], w_glu_b, w_out_b, conv_w, conv_b3, g_ffn3,
                             batch, grp["seg"], l)
            grp["x"] = _ffn(h2, x1, grp["mod"], w_gate_b, w_up_b, w_down_b, g_final[None], l, batch,
                            final=(l == DEPTH - 1))
            if grp["s0"] is None:
                finals.append(_final_states(fin))

    new_state = jnp.stack(finals, axis=1).astype(x_prompt.dtype)
    return (groups[0]["x"], groups[1]["x"], new_state)
```

```python
import functools

import jax
import jax.numpy as jnp
from jax import lax
from jax.experimental import pallas as pl
from jax.experimental.pallas import tpu as pltpu

D_MODEL = 2048
DEPTH = 2
GRID_W = 64
D_SSM = 1024
D_CONV = 1024
SSM_H = 16
SSM_GROUPS = 64
SSM_STATE = 64
N_DIR = 2
D_FF = 5632
N_MOD = 6
EPS = 1e-6
LAM_RE_MAX = -1e-4

SUBLANES = 8
LANES = 128
GROUPS_PER_BLOCK = LANES // SSM_H
N_GROUP_BLOCKS = SSM_GROUPS // GROUPS_PER_BLOCK
PAIRS = GROUPS_PER_BLOCK // 2
STATE_LANES = GROUPS_PER_BLOCK * 2 * SSM_STATE
SCAN_STEPS = 64
T_BLK = 4
VMEM_LIMIT = 56 * 1024 * 1024

TM_INPROJ = 1024
TM_MIXOUT = 512
TM_FFN = 512
TN_INPROJ = 1024
TF_FFN = 512

F32 = jnp.float32
BF16 = jnp.bfloat16


def _params(*sem):
    return pltpu.CompilerParams(dimension_semantics=sem, vmem_limit_bytes=VMEM_LIMIT)


def _resident(shape, index_map):
    return pl.BlockSpec(shape, index_map, pipeline_mode=pl.Buffered(1))


def _rows8(a):
    return a.reshape(a.shape[0] // SUBLANES, SUBLANES, a.shape[1])


def _norm_mod(x, gain, scale, shift):
    ms = jnp.mean(x * x, axis=-1, keepdims=True)
    y = x * lax.rsqrt(ms + EPS) * gain
    h = _rows8(y) * (1.0 + scale)[None] + shift[None]
    return h.reshape(x.shape)


def _cast_kernel(w_ref, o_ref):
    o_ref[...] = w_ref[...].astype(BF16).reshape(o_ref.shape)


def _cast_col_tiles(w, tn, parts=1):
    _, k, n = w.shape
    nt = n // tn
    q = tn // parts
    return pl.pallas_call(
        _cast_kernel,
        grid=(DEPTH, nt, parts),
        in_specs=[pl.BlockSpec((1, k, q), lambda l, j, p: (l, 0, p * nt + j))],
        out_specs=pl.BlockSpec((1, 1, k, q), lambda l, j, p: (l, j, 0, p)),
        out_shape=jax.ShapeDtypeStruct((DEPTH, nt, k, tn), BF16),
        compiler_params=_params("parallel", "parallel", "parallel"),
        name="cast_col_tiles",
    )(w)


def _cast_rows(w, tk):
    _, k, n = w.shape
    return pl.pallas_call(
        _cast_kernel,
        grid=(DEPTH, k // tk),
        in_specs=[pl.BlockSpec((1, tk, n), lambda l, i: (l, i, 0))],
        out_specs=pl.BlockSpec((1, tk, n), lambda l, i: (l, i, 0)),
        out_shape=jax.ShapeDtypeStruct((DEPTH, k, n), BF16),
        compiler_params=_params("parallel", "parallel"),
        name="cast_rows",
    )(w)


def _mod_kernel(cond_ref, w_ref, b_ref, o_ref):
    s = jax.nn.silu(cond_ref[...]).astype(BF16)
    o_ref[0] = jnp.dot(s, w_ref[0].astype(BF16), preferred_element_type=F32) + b_ref[0]


def _modulation(cond, w_ada, b_ada):
    tn = 1024
    n = N_MOD * D_MODEL
    return pl.pallas_call(
        _mod_kernel,
        grid=(DEPTH, n // tn),
        in_specs=[
            pl.BlockSpec((16, D_MODEL), lambda l, j: (0, 0)),
            pl.BlockSpec((1, D_MODEL, tn), lambda l, j: (l, 0, j)),
            pl.BlockSpec((1, 1, tn), lambda l, j: (l, 0, j)),
        ],
        out_specs=pl.BlockSpec((1, 16, tn), lambda l, j: (l, 0, j)),
        out_shape=jax.ShapeDtypeStruct((DEPTH, 16, n), F32),
        compiler_params=_params("parallel", "parallel"),
        name="adaln_modulation",
    )(cond, w_ada, b_ada.reshape(DEPTH, 1, n))


def _inproj_kernel(x_ref, mod_ref, g_ref, w_ref, *out_refs, batch_major):
    if batch_major:
        xt_ref, u_ref, gb_ref, zc_ref, h_scr = out_refs
    else:
        u_ref, gb_ref, zc_ref, h_scr = out_refs

    @pl.when(pl.program_id(1) == 0)
    def _():
        if batch_major:
            x = jnp.concatenate([x_ref[:, t, :] for t in range(x_ref.shape[1])], axis=0)
            xt_ref[...] = x
        else:
            x = x_ref[...]
        h = _norm_mod(x, g_ref[0], mod_ref[0, 1], mod_ref[0, 0])
        h_scr[...] = h.astype(BF16)

    z = jnp.dot(h_scr[...], w_ref[0, 0], preferred_element_type=F32)
    q = z.shape[1] // 4
    u_ref[...] = z[:, :q]
    gb_ref[...] = z[:, q:2 * q]
    zc_ref[...] = z[:, 2 * q:3 * q] * z[:, 3 * q:]


def _inproj(x, mod, g_mix, w_in_tiled, layer, batch_major):
    tn = TN_INPROJ
    tm = TM_INPROJ // 2 if batch_major else TM_INPROJ
    q = tn // 4
    if batch_major:
        b, l, _ = x.shape
        r = b * l
        x_spec = pl.BlockSpec((b, tm // b, D_MODEL), lambda i, j: (0, i, 0))
    else:
        r = x.shape[0]
        x_spec = pl.BlockSpec((tm, D_MODEL), lambda i, j: (i, 0))
    out = jax.ShapeDtypeStruct((r, D_SSM), F32)
    out_specs = [pl.BlockSpec((tm, q), lambda i, j: (i, j))] * 3
    out_shape = [out, out, out]
    if batch_major:
        out_specs = [pl.BlockSpec((tm, D_MODEL), lambda i, j: (i, 0))] + out_specs
        out_shape = [jax.ShapeDtypeStruct((r, D_MODEL), F32)] + out_shape
    return pl.pallas_call(
        functools.partial(_inproj_kernel, batch_major=batch_major),
        grid=(r // tm, 4 * D_SSM // tn),
        in_specs=[
            x_spec,
            pl.BlockSpec((1, N_MOD, SUBLANES, D_MODEL), lambda i, j: (layer, 0, 0, 0)),
            pl.BlockSpec((1, 1, D_MODEL), lambda i, j: (layer, 0, 0)),
            pl.BlockSpec((1, 1, D_MODEL, tn), lambda i, j: (layer, j, 0, 0)),
        ],
        out_specs=out_specs,
        out_shape=out_shape,
        scratch_shapes=[pltpu.VMEM((tm, D_MODEL), BF16)],
        compiler_params=_params("parallel", "arbitrary"),
        name="inproj",
    )(x, mod, g_mix, w_in_tiled)


def _s5_kernel(*refs, batch, has_init):
    if has_init:
        u_ref, bc_ref, cc_ref, a_ref, d_ref, s0_ref, y_ref, fin_ref, sf, sb, xs = refs
    else:
        u_ref, bc_ref, cc_ref, a_ref, d_ref, y_ref, fin_ref, sf, sb, xs = refs
        s0_ref = None
    rows = u_ref.shape[0]
    rc = SCAN_STEPS * batch
    n_chunks = rows // rc
    nblk = SCAN_STEPS // T_BLK
    rblk = nblk * batch
    halves = batch // SUBLANES
    tl = T_BLK * LANES

    y_ref[...] = d_ref[0] * u_ref[...]
    if has_init:
        xs[...] = s0_ref[0]
    else:
        xs[...] = jnp.zeros_like(xs)

    def gather(r0):
        uc = u_ref[pl.ds(r0, rc), :].reshape(nblk, T_BLK, batch, LANES)
        return jnp.concatenate([uc[:, i].reshape(rblk, LANES) for i in range(T_BLK)], axis=1).astype(BF16)

    def scatter_add(r0, yb):
        parts = [yb[:, i * LANES:(i + 1) * LANES].reshape(nblk, 1, batch, LANES) for i in range(T_BLK)]
        y_ref[pl.ds(r0, rc), :] += jnp.concatenate(parts, axis=1).reshape(rc, LANES)

    def chunk(ci, carry):
        rf = pl.multiple_of(ci * rc, rc)
        rb = pl.multiple_of((n_chunks - 1 - ci) * rc, rc)
        lf = gather(rf)
        lb = gather(rb)
        sf[...] = jnp.dot(lf, bc_ref[0, :, :STATE_LANES], preferred_element_type=F32)
        sb[...] = jnp.dot(lb, bc_ref[0, :, STATE_LANES:], preferred_element_type=F32)
        for half in range(halves):
            hs = slice(half * SUBLANES, (half + 1) * SUBLANES)
            for pair in range(PAIRS):
                re = slice(pair * 2 * LANES, pair * 2 * LANES + LANES)
                im = slice(pair * 2 * LANES + LANES, (pair + 1) * 2 * LANES)
                for d, s in ((0, sf), (1, sb)):
                    al = slice(d * PAIRS * LANES + pair * LANES, d * PAIRS * LANES + (pair + 1) * LANES)
                    ar = a_ref[0, 0, :, al]
                    ai = a_ref[0, 1, :, al]
                    xr = xs[d, hs, re]
                    xi = xs[d, hs, im]
                    for t in range(nblk):
                        blk = t if d == 0 else nblk - 1 - t
                        rs = slice(blk * batch + half * SUBLANES, blk * batch + (half + 1) * SUBLANES)
                        sr = s[rs, re]
                        si = s[rs, im]
                        s[rs, re] = xr
                        s[rs, im] = xi
                        xr, xi = ar * xr - ai * xi + sr, ar * xi + ai * xr + si
                    xs[d, hs, re] = xr
                    xs[d, hs, im] = xi
        yf = jnp.dot(jnp.concatenate([sf[...].astype(BF16), lf], axis=1), cc_ref[0, :STATE_LANES + tl, :],
                     preferred_element_type=F32)
        yb = jnp.dot(sb[...].astype(BF16), cc_ref[0, STATE_LANES + tl:, :], preferred_element_type=F32)
        scatter_add(rf, yf)
        scatter_add(rb, yb)
        return carry

    lax.fori_loop(0, n_chunks, chunk, 0)
    fin_ref[0] = xs[...]


def _s5(u, bc, cc, a, d_skip, s0, batch, layer):
    r = u.shape[0]
    rblk = (SCAN_STEPS // T_BLK) * batch
    tl = T_BLK * LANES
    has_init = s0 is not None
    in_specs = [
        pl.BlockSpec((r, LANES), lambda g: (0, g)),
        pl.BlockSpec((1, tl, N_DIR * STATE_LANES), lambda g: (g, 0, 0)),
        pl.BlockSpec((1, N_DIR * STATE_LANES + tl, tl), lambda g: (g, 0, 0)),
        pl.BlockSpec((1, 2, SUBLANES, N_DIR * PAIRS * LANES), lambda g: (g, 0, 0, 0)),
        pl.BlockSpec((1, 1, LANES), lambda g: (layer, 0, g)),
    ]
    args = [u, bc, cc, a, d_skip]
    if has_init:
        in_specs.append(pl.BlockSpec((1, N_DIR, batch, STATE_LANES), lambda g: (g, 0, 0, 0)))
        args.append(s0)
    return pl.pallas_call(
        functools.partial(_s5_kernel, batch=batch, has_init=has_init),
        grid=(N_GROUP_BLOCKS,),
        in_specs=in_specs,
        out_specs=[
            pl.BlockSpec((r, LANES), lambda g: (0, g)),
            pl.BlockSpec((1, N_DIR, batch, STATE_LANES), lambda g: (g, 0, 0, 0)),
        ],
        out_shape=[
            jax.ShapeDtypeStruct((r, D_SSM), F32),
            jax.ShapeDtypeStruct((N_GROUP_BLOCKS, N_DIR, batch, STATE_LANES), F32),
        ],
        scratch_shapes=[
            pltpu.VMEM((rblk, STATE_LANES), F32),
            pltpu.VMEM((rblk, STATE_LANES), F32),
            pltpu.VMEM((N_DIR, batch, STATE_LANES), F32),
        ],
        compiler_params=_params("parallel"),
        name="s5_mixer",
    )(*args)


def _mixout_kernel(y_ref, gb_ref, zc_ref, zp_ref, zn_ref, x_ref, mod_ref, wglu_ref, wout_ref,
                   cw_ref, cb_ref, g_ref, x1_ref, h2_ref, *, batch, seg):
    tm = x_ref.shape[0]
    tl = tm // batch
    l0 = pl.program_id(0) * tl
    prev_ok = (l0 % seg != 0).astype(F32)
    next_ok = ((l0 + tl) % seg != 0).astype(F32)

    y = jax.nn.gelu(y_ref[...])
    y_ssm = y * jax.nn.sigmoid(jnp.dot(y.astype(BF16), wglu_ref[0], preferred_element_type=F32))

    zc = zc_ref[...]
    prev = jnp.concatenate([zp_ref[...] * prev_ok, zc[:tm - batch]], axis=0)
    nxt = jnp.concatenate([zc[batch:], zn_ref[...] * next_ok], axis=0)
    conv = prev * cw_ref[0, 0:1, :] + zc * cw_ref[0, 1:2, :] + nxt * cw_ref[0, 2:3, :] + cb_ref[0]
    y_conv = gb_ref[...] * conv

    out = jnp.dot(y_ssm.astype(BF16), wout_ref[0, :D_SSM, :], preferred_element_type=F32)
    out += jnp.dot(y_conv.astype(BF16), wout_ref[0, D_SSM:, :], preferred_element_type=F32)
    x1 = (_rows8(x_ref[...]) + mod_ref[0, 2][None] * _rows8(out)).reshape(tm, D_MODEL)
    x1_ref[...] = x1
    h2_ref[...] = _norm_mod(x1, g_ref[0], mod_ref[0, 4], mod_ref[0, 3]).astype(BF16)


def _mixout(y, gb, zc, x, mod, w_glu, w_out, conv_w, conv_b, g_ffn, batch, seg, layer):
    r = x.shape[0]
    tm = TM_MIXOUT
    assert seg % (tm // batch) == 0
    hb = tm // batch
    n_hb = r // batch
    row = lambda i: (i, 0)
    lay = lambda i: (layer, 0, 0)
    return pl.pallas_call(
        functools.partial(_mixout_kernel, batch=batch, seg=seg),
        grid=(r // tm,),
        in_specs=[
            pl.BlockSpec((tm, D_SSM), row),
            pl.BlockSpec((tm, D_CONV), row),
            pl.BlockSpec((tm, D_CONV), row),
            pl.BlockSpec((batch, D_CONV), lambda i: (jnp.maximum(i * hb - 1, 0), 0)),
            pl.BlockSpec((batch, D_CONV), lambda i: (jnp.minimum((i + 1) * hb, n_hb - 1), 0)),
            pl.BlockSpec((tm, D_MODEL), row),
            _resident((1, N_MOD, SUBLANES, D_MODEL), lambda i: (layer, 0, 0, 0)),
            _resident((1, D_SSM, D_SSM), lay),
            _resident((1, D_MODEL, D_MODEL), lay),
            _resident((1, 3, D_CONV), lay),
            _resident((1, 1, D_CONV), lay),
            _resident((1, 1, D_MODEL), lay),
        ],
        out_specs=[pl.BlockSpec((tm, D_MODEL), row), pl.BlockSpec((tm, D_MODEL), row)],
        out_shape=[jax.ShapeDtypeStruct((r, D_MODEL), F32), jax.ShapeDtypeStruct((r, D_MODEL), BF16)],
        compiler_params=_params("parallel"),
        name="mixer_out",
    )(y, gb, zc, zc, zc, x, mod, w_glu, w_out, conv_w, conv_b, g_ffn)


def _ffn_kernel(h_ref, x_ref, mod_ref, wg_ref, wu_ref, wd_ref, gf_ref, o_ref, acc, *, final):
    k = pl.program_id(1)

    @pl.when(k == 0)
    def _():
        acc[...] = jnp.zeros_like(acc)

    h = h_ref[...]
    gate = jnp.dot(h, wg_ref[0, 0], preferred_element_type=F32)
    up = jnp.dot(h, wu_ref[0, 0], preferred_element_type=F32)
    act = (jax.nn.silu(gate) * up).astype(BF16)
    acc[...] += jnp.dot(act, wd_ref[0], preferred_element_type=F32)

    @pl.when(k == pl.num_programs(1) - 1)
    def _():
        x2 = (_rows8(x_ref[...]) + mod_ref[0, 5][None] * _rows8(acc[...])).reshape(x_ref.shape)
        if final:
            ms = jnp.mean(x2 * x2, axis=-1, keepdims=True)
            x2 = x2 * lax.rsqrt(ms + EPS) * gf_ref[...]
            b = o_ref.shape[0]
            for t in range(o_ref.shape[1]):
                o_ref[:, t, :] = x2[t * b:(t + 1) * b, :]
        else:
            o_ref[...] = x2


def _ffn(h2, x1, mod, w_gate, w_up, w_down, g_final, layer, batch, final):
    r = x1.shape[0]
    tm, tf = TM_FFN, TF_FFN
    if final:
        out_spec = pl.BlockSpec((batch, tm // batch, D_MODEL), lambda i, k: (0, i, 0))
        out_shape = jax.ShapeDtypeStruct((batch, r // batch, D_MODEL), F32)
    else:
        out_spec = pl.BlockSpec((tm, D_MODEL), lambda i, k: (i, 0))
        out_shape = jax.ShapeDtypeStruct((r, D_MODEL), F32)
    return pl.pallas_call(
        functools.partial(_ffn_kernel, final=final),
        grid=(r // tm, D_FF // tf),
        in_specs=[
            pl.BlockSpec((tm, D_MODEL), lambda i, k: (i, 0)),
            pl.BlockSpec((tm, D_MODEL), lambda i, k: (i, 0)),
            pl.BlockSpec((1, N_MOD, SUBLANES, D_MODEL), lambda i, k: (layer, 0, 0, 0)),
            pl.BlockSpec((1, 1, D_MODEL, tf), lambda i, k: (layer, k, 0, 0)),
            pl.BlockSpec((1, 1, D_MODEL, tf), lambda i, k: (layer, k, 0, 0)),
            pl.BlockSpec((1, tf, D_MODEL), lambda i, k: (layer, k, 0)),
            pl.BlockSpec((1, D_MODEL), lambda i, k: (0, 0)),
        ],
        out_specs=out_spec,
        out_shape=out_shape,
        scratch_shapes=[pltpu.VMEM((tm, D_MODEL), F32)],
        compiler_params=_params("parallel", "arbitrary"),
        name="ffn",
    )(h2, x1, mod, w_gate, w_up, w_down, g_final)


def _cpow(lr, li, dt, k):
    m = jnp.exp(lr * dt * k)
    return m * jnp.cos(li * dt * k), m * jnp.sin(li * dt * k)


def _s5_matrices(lam_re, lam_im, log_dt, b_re, b_im, c_re, c_im):
    hi = lax.Precision.HIGHEST
    t = T_BLK
    lr = jnp.minimum(lam_re, LAM_RE_MAX)
    li = lam_im
    dt = jnp.exp(log_dt)[..., None]
    ar, ai = _cpow(lr, li, dt, 1.0)
    den = lr * lr + li * li
    qr = ((ar - 1.0) * lr + ai * li) / den
    qi = (ai * lr - (ar - 1.0) * li) / den
    bbr = qr[..., None] * b_re - qi[..., None] * b_im
    bbi = qr[..., None] * b_im + qi[..., None] * b_re

    steps = jnp.arange(t, dtype=F32)
    e_in = jnp.stack([t - 1.0 - steps, steps], axis=0)[:, :, None, None]
    e_out = jnp.stack([steps + 1.0, t - steps], axis=0)[:, :, None, None]
    e_lag = jnp.stack([steps, steps], axis=0)[:, :, None, None]
    lr4, li4, dt4 = lr[:, None], li[:, None], dt[:, None]
    pr_in, pi_in = _cpow(lr4, li4, dt4, e_in)
    pr_out, pi_out = _cpow(lr4, li4, dt4, e_out)
    pr_lag, pi_lag = _cpow(lr4, li4, dt4, e_lag)

    nb, gp = N_GROUP_BLOCKS, GROUPS_PER_BLOCK
    sel = jnp.eye(gp, dtype=F32).reshape(gp, PAIRS, 2)

    cr = pr_in[..., None] * bbr[:, None] - pi_in[..., None] * bbi[:, None]
    ci = pr_in[..., None] * bbi[:, None] + pi_in[..., None] * bbr[:, None]
    tb = jnp.stack([cr, ci], axis=3).reshape(N_DIR, t, nb, gp, 2, SSM_STATE, SSM_H)
    bc = jnp.einsum("dibgrph,gqs->bighdqrsp", tb, sel).reshape(nb, t * LANES, N_DIR * STATE_LANES)

    def c_times(pr, pi):
        pr, pi = pr[:, :, :, None, :], pi[:, :, :, None, :]
        return c_re[:, None] * pr - c_im[:, None] * pi, c_re[:, None] * pi + c_im[:, None] * pr

    kr, ki = c_times(pr_out, pi_out)
    tc = jnp.stack([kr, -ki], axis=3).reshape(N_DIR, t, nb, gp, 2, SSM_H, SSM_STATE)
    ccs = jnp.einsum("dibgrhp,gqs->bdqrspigh", tc, sel).reshape(nb, N_DIR, STATE_LANES, t * LANES)

    lgr, lgi = c_times(pr_lag, pi_lag)
    klag = (jnp.einsum("dlgop,dgph->dlgoh", lgr, bbr, precision=hi)
            - jnp.einsum("dlgop,dgph->dlgoh", lgi, bbi, precision=hi))
    idx = jnp.arange(t)
    lag = idx[None, :] - idx[:, None]
    m5 = lambda m: m[:, :, None, None, None]
    kt = (jnp.where(m5(lag >= 0), klag[0][jnp.clip(lag, 0, t - 1)], 0.0)
          + jnp.where(m5(lag <= 0), klag[1][jnp.clip(-lag, 0, t - 1)], 0.0))
    kt = kt.reshape(t, t, nb, gp, SSM_H, SSM_H)
    toep = jnp.einsum("jibgoh,gk->bjghiko", kt, jnp.eye(gp, dtype=F32)).reshape(nb, t * LANES, t * LANES)

    cc = jnp.concatenate([ccs[:, 0], toep, ccs[:, 1]], axis=1)
    atr, ati = _cpow(lr, li, dt, float(t))
    a = jnp.stack([atr, ati], axis=0)
    a = a.reshape(2, N_DIR, nb, PAIRS * LANES).transpose(2, 0, 1, 3).reshape(nb, 2, 1, N_DIR * PAIRS * LANES)
    a = jnp.broadcast_to(a, (nb, 2, SUBLANES, N_DIR * PAIRS * LANES))
    return bc.astype(BF16), cc.astype(BF16), a


def _init_states(st):
    b = st.shape[0]
    s = st.reshape(b, N_DIR, 2, N_GROUP_BLOCKS, PAIRS, 2, SSM_STATE)
    return s.transpose(3, 1, 0, 4, 2, 5, 6).reshape(N_GROUP_BLOCKS, N_DIR, b, STATE_LANES)


def _final_states(fin):
    b = fin.shape[2]
    s = fin.reshape(N_GROUP_BLOCKS, N_DIR, b, PAIRS, 2, 2, SSM_STATE)
    return s.transpose(2, 1, 4, 0, 3, 5, 6).reshape(b, N_DIR, 2, SSM_GROUPS, SSM_STATE)


def kernel(x_prompt, x_sample, state_ssm, c, c_ctx, w_ada, b_ada, g_mix, w_in, ssm_lam_re, ssm_lam_im, ssm_log_dt, ssm_b_re, ssm_b_im, ssm_c_re, ssm_c_im, ssm_d, w_glu, conv_w, conv_b, w_out, g_ffn, w_gate, w_up, w_down, g_final):
    n_ctx, l_ctx, _ = x_prompt.shape
    n_dec, l_dec, _ = x_sample.shape

    cond = jnp.zeros((16, D_MODEL), F32).at[0].set(c_ctx).at[1:1 + n_dec].set(c)
    mod = _modulation(cond, w_ada, b_ada).reshape(DEPTH, 16, N_MOD, D_MODEL)
    mod_ctx = jnp.broadcast_to(mod[:, 0, :, None, :], (DEPTH, N_MOD, SUBLANES, D_MODEL))
    mod_dec = mod[:, 1:1 + n_dec].transpose(0, 2, 1, 3)

    w_in_t = _cast_col_tiles(w_in, TN_INPROJ, parts=4)
    w_glu_b = _cast_rows(w_glu, 512)
    w_out_b = _cast_rows(w_out, 512)
    w_gate_b = _cast_col_tiles(w_gate, TF_FFN)
    w_up_b = _cast_col_tiles(w_up, TF_FFN)
    w_down_b = _cast_rows(w_down, TF_FFN)
    g_mix3 = g_mix[:, None, :]
    g_ffn3 = g_ffn[:, None, :]
    conv_b3 = conv_b[:, None, :]
    ssm_d3 = ssm_d[:, None, :]

    groups = [
        dict(x=x_prompt, batch=n_ctx, seg=l_ctx, mod=mod_ctx, s0=None),
        dict(x=x_sample, batch=n_dec, seg=GRID_W, mod=mod_dec, s0=state_ssm.astype(F32)),
    ]

    finals = []
    for l in range(DEPTH):
        bc, cc, a = _s5_matrices(ssm_lam_re[l], ssm_lam_im[l], ssm_log_dt[l], ssm_b_re[l], ssm_b_im[l],
                                 ssm_c_re[l], ssm_c_im[l])
        for grp in groups:
            batch = grp["batch"]
            s0 = None if grp["s0"] is None else _init_states(grp["s0"][:, l])
            if l == 0:
                x, u, gb, zc = _inproj(grp["x"], grp["mod"], g_mix3, w_in_t, l, batch_major=True)
            else:
                x = grp["x"]
                u, gb, zc = _inproj(x, grp["mod"], g_mix3, w_in_t, l, batch_major=False)
            y, fin = _s5(u, bc, cc, a, ssm_d3, s0, batch, l)
            x1, h2 = _mixout(y, gb, zc, x, grp["mod"], w_glu_b, w_out_b, conv_w, conv_b3, g_ffn3,
                             batch, grp["seg"], l)
            grp["x"] = _ffn(h2, x1, grp["mod"], w_gate_b, w_up_b, w_down_b, g_final[None], l, batch,
                            final=(l == DEPTH - 1))
            if grp["s0"] is None:
                finals.append(_final_states(fin))

    new_state = jnp.stack(finals, axis=1).astype(x_prompt.dtype)
    return (groups[0]["x"], groups[1]["x"], new_state)
```

```python
import functools

import jax
import jax.numpy as jnp
from jax import lax
from jax.experimental import pallas as pl
from jax.experimental.pallas import tpu as pltpu

D_MODEL = 2048
DEPTH = 2
GRID_W = 64
D_SSM = 1024
D_CONV = 1024
SSM_H = 16
SSM_GROUPS = 64
SSM_STATE = 64
N_DIR = 2
D_FF = 5632
N_MOD = 6
EPS = 1e-6
LAM_RE_MAX = -1e-4

SUBLANES = 8
LANES = 128
GROUPS_PER_BLOCK = LANES // SSM_H
N_GROUP_BLOCKS = SSM_GROUPS // GROUPS_PER_BLOCK
PAIRS = GROUPS_PER_BLOCK // 2
STATE_LANES = GROUPS_PER_BLOCK * 2 * SSM_STATE
SCAN_STEPS = 64
T_BLK = 4
VMEM_LIMIT = 56 * 1024 * 1024

TM_INPROJ = 1024
TM_MIXOUT = 512
TM_FFN = 512
TN_INPROJ = 1024
TF_FFN = 512

F32 = jnp.float32
BF16 = jnp.bfloat16


def _params(*sem):
    return pltpu.CompilerParams(dimension_semantics=sem, vmem_limit_bytes=VMEM_LIMIT)


def _resident(shape, index_map):
    return pl.BlockSpec(shape, index_map, pipeline_mode=pl.Buffered(1))


def _rows8(a):
    return a.reshape(a.shape[0] // SUBLANES, SUBLANES, a.shape[1])


def _norm_mod(x, gain, scale, shift):
    ms = jnp.mean(x * x, axis=-1, keepdims=True)
    y = x * lax.rsqrt(ms + EPS) * gain
    h = _rows8(y) * (1.0 + scale)[None] + shift[None]
    return h.reshape(x.shape)


def _cast_kernel(w_ref, o_ref):
    o_ref[...] = w_ref[...].astype(BF16).reshape(o_ref.shape)


def _cast_col_tiles(w, tn, parts=1):
    _, k, n = w.shape
    nt = n // tn
    q = tn // parts
    return pl.pallas_call(
        _cast_kernel,
        grid=(DEPTH, nt, parts),
        in_specs=[pl.BlockSpec((1, k, q), lambda l, j, p: (l, 0, p * nt + j))],
        out_specs=pl.BlockSpec((1, 1, k, q), lambda l, j, p: (l, j, 0, p)),
        out_shape=jax.ShapeDtypeStruct((DEPTH, nt, k, tn), BF16),
        compiler_params=_params("parallel", "parallel", "parallel"),
        name="cast_col_tiles",
    )(w)


def _cast_rows(w, tk):
    _, k, n = w.shape
    return pl.pallas_call(
        _cast_kernel,
        grid=(DEPTH, k // tk),
        in_specs=[pl.BlockSpec((1, tk, n), lambda l, i: (l, i, 0))],
        out_specs=pl.BlockSpec((1, tk, n), lambda l, i: (l, i, 0)),
        out_shape=jax.ShapeDtypeStruct((DEPTH, k, n), BF16),
        compiler_params=_params("parallel", "parallel"),
        name="cast_rows",
    )(w)


def _mod_kernel(cond_ref, w_ref, b_ref, o_ref):
    s = jax.nn.silu(cond_ref[...]).astype(BF16)
    o_ref[0] = jnp.dot(s, w_ref[0].astype(BF16), preferred_element_type=F32) + b_ref[0]


def _modulation(cond, w_ada, b_ada):
    tn = 1024
    n = N_MOD * D_MODEL
    return pl.pallas_call(
        _mod_kernel,
        grid=(DEPTH, n // tn),
        in_specs=[
            pl.BlockSpec((16, D_MODEL), lambda l, j: (0, 0)),
            pl.BlockSpec((1, D_MODEL, tn), lambda l, j: (l, 0, j)),
            pl.BlockSpec((1, 1, tn), lambda l, j: (l, 0, j)),
        ],
        out_specs=pl.BlockSpec((1, 16, tn), lambda l, j: (l, 0, j)),
        out_shape=jax.ShapeDtypeStruct((DEPTH, 16, n), F32),
        compiler_params=_params("parallel", "parallel"),
        name="adaln_modulation",
    )(cond, w_ada, b_ada.reshape(DEPTH, 1, n))


def _inproj_kernel(x_ref, mod_ref, g_ref, w_ref, *out_refs, batch_major):
    if batch_major:
        xt_ref, u_ref, gb_ref, zc_ref, h_scr = out_refs
    else:
        u_ref, gb_ref, zc_ref, h_scr = out_refs

    @pl.when(pl.program_id(1) == 0)
    def _():
        if batch_major:
            x = jnp.concatenate([x_ref[:, t, :] for t in range(x_ref.shape[1])], axis=0)
            xt_ref[...] = x
        else:
            x = x_ref[...]
        h = _norm_mod(x, g_ref[0], mod_ref[0, 1], mod_ref[0, 0])
        h_scr[...] = h.astype(BF16)

    z = jnp.dot(h_scr[...], w_ref[0, 0], preferred_element_type=F32)
    q = z.shape[1] // 4
    u_ref[...] = z[:, :q]
    gb_ref[...] = z[:, q:2 * q]
    zc_ref[...] = z[:, 2 * q:3 * q] * z[:, 3 * q:]


def _inproj(x, mod, g_mix, w_in_tiled, layer, batch_major):
    tn = TN_INPROJ
    tm = TM_INPROJ // 2 if batch_major else TM_INPROJ
    q = tn // 4
    if batch_major:
        b, l, _ = x.shape
        r = b * l
        x_spec = pl.BlockSpec((b, tm // b, D_MODEL), lambda i, j: (0, i, 0))
    else:
        r = x.shape[0]
        x_spec = pl.BlockSpec((tm, D_MODEL), lambda i, j: (i, 0))
    out = jax.ShapeDtypeStruct((r, D_SSM), F32)
    out_specs = [pl.BlockSpec((tm, q), lambda i, j: (i, j))] * 3
    out_shape = [out, out, out]
    if batch_major:
        out_specs = [pl.BlockSpec((tm, D_MODEL), lambda i, j: (i, 0))] + out_specs
        out_shape = [jax.ShapeDtypeStruct((r, D_MODEL), F32)] + out_shape
    return pl.pallas_call(
        functools.partial(_inproj_kernel, batch_major=batch_major),
        grid=(r // tm, 4 * D_SSM // tn),
        in_specs=[
            x_spec,
            pl.BlockSpec((1, N_MOD, SUBLANES, D_MODEL), lambda i, j: (layer, 0, 0, 0)),
            pl.BlockSpec((1, 1, D_MODEL), lambda i, j: (layer, 0, 0)),
            pl.BlockSpec((1, 1, D_MODEL, tn), lambda i, j: (layer, j, 0, 0)),
        ],
        out_specs=out_specs,
        out_shape=out_shape,
        scratch_shapes=[pltpu.VMEM((tm, D_MODEL), BF16)],
        compiler_params=_params("parallel", "arbitrary"),
        name="inproj",
    )(x, mod, g_mix, w_in_tiled)


def _s5_kernel(*refs, batch, has_init):
    if has_init:
        u_ref, bc_ref, cct_ref, toep_ref, a_ref, d_ref, s0_ref, y_ref, fin_ref, sf, sb, xs = refs
    else:
        u_ref, bc_ref, cct_ref, toep_ref, a_ref, d_ref, y_ref, fin_ref, sf, sb, xs = refs
        s0_ref = None
    rows = u_ref.shape[0]
    rc = SCAN_STEPS * batch
    n_chunks = rows // rc
    nblk = SCAN_STEPS // T_BLK
    rblk = nblk * batch
    halves = batch // SUBLANES
    tl = T_BLK * LANES

    y_ref[...] = d_ref[0] * u_ref[...]
    if has_init:
        xs[...] = s0_ref[0]
    else:
        xs[...] = jnp.zeros_like(xs)

    def gather(r0):
        uc = u_ref[pl.ds(r0, rc), :].reshape(nblk, T_BLK, batch, LANES)
        return jnp.concatenate([uc[:, i].reshape(rblk, LANES) for i in range(T_BLK)], axis=1).astype(BF16)

    def scatter_add(r0, yb):
        parts = [yb[:, i * LANES:(i + 1) * LANES].reshape(nblk, 1, batch, LANES) for i in range(T_BLK)]
        y_ref[pl.ds(r0, rc), :] += jnp.concatenate(parts, axis=1).reshape(rc, LANES)

    def chunk(ci, carry):
        rf = pl.multiple_of(ci * rc, rc)
        rb = pl.multiple_of((n_chunks - 1 - ci) * rc, rc)
        lf = gather(rf)
        lb = gather(rb)
        sf[...] = jnp.dot(lf, bc_ref[0, :, :STATE_LANES], preferred_element_type=F32)
        sb[...] = jnp.dot(lb, bc_ref[0, :, STATE_LANES:], preferred_element_type=F32)
        for half in range(halves):
            hs = slice(half * SUBLANES, (half + 1) * SUBLANES)
            for pair in range(PAIRS):
                re = slice(pair * LANES, (pair + 1) * LANES)
                im = slice(STATE_LANES // 2 + pair * LANES, STATE_LANES // 2 + (pair + 1) * LANES)
                for d, s in ((0, sf), (1, sb)):
                    al = slice(d * PAIRS * LANES + pair * LANES, d * PAIRS * LANES + (pair + 1) * LANES)
                    ar = a_ref[0, 0, :, al]
                    ai = a_ref[0, 1, :, al]
                    xr = xs[d, hs, re]
                    xi = xs[d, hs, im]
                    for t in range(nblk):
                        blk = t if d == 0 else nblk - 1 - t
                        rs = slice(blk * batch + half * SUBLANES, blk * batch + (half + 1) * SUBLANES)
                        sr = s[rs, re]
                        si = s[rs, im]
                        s[rs, re] = xr
                        s[rs, im] = xi
                        xr, xi = ar * xr - ai * xi + sr, ar * xi + ai * xr + si
                    xs[d, hs, re] = xr
                    xs[d, hs, im] = xi
        nt = (((1,), (1,)), ((), ()))
        yf = lax.dot_general(sf[...].astype(BF16), cct_ref[0, :, :STATE_LANES], nt, preferred_element_type=F32)
        yf += jnp.dot(lf, toep_ref[0], preferred_element_type=F32)
        yb = lax.dot_general(sb[...].astype(BF16), cct_ref[0, :, STATE_LANES:], nt, preferred_element_type=F32)
        scatter_add(rf, yf)
        scatter_add(rb, yb)
        return carry

    lax.fori_loop(0, n_chunks, chunk, 0)
    fin_ref[0] = xs[...]


def _s5(u, bc, cct, toep, a, d_skip, s0, batch, layer):
    r = u.shape[0]
    rblk = (SCAN_STEPS // T_BLK) * batch
    tl = T_BLK * LANES
    has_init = s0 is not None
    in_specs = [
        pl.BlockSpec((r, LANES), lambda g: (0, g)),
        pl.BlockSpec((1, tl, N_DIR * STATE_LANES), lambda g: (g, 0, 0)),
        pl.BlockSpec((1, tl, N_DIR * STATE_LANES), lambda g: (g, 0, 0)),
        pl.BlockSpec((1, tl, tl), lambda g: (g, 0, 0)),
        pl.BlockSpec((1, 2, SUBLANES, N_DIR * PAIRS * LANES), lambda g: (g, 0, 0, 0)),
        pl.BlockSpec((1, 1, LANES), lambda g: (layer, 0, g)),
    ]
    args = [u, bc, cct, toep, a, d_skip]
    if has_init:
        in_specs.append(pl.BlockSpec((1, N_DIR, batch, STATE_LANES), lambda g: (g, 0, 0, 0)))
        args.append(s0)
    return pl.pallas_call(
        functools.partial(_s5_kernel, batch=batch, has_init=has_init),
        grid=(N_GROUP_BLOCKS,),
        in_specs=in_specs,
        out_specs=[
            pl.BlockSpec((r, LANES), lambda g: (0, g)),
            pl.BlockSpec((1, N_DIR, batch, STATE_LANES), lambda g: (g, 0, 0, 0)),
        ],
        out_shape=[
            jax.ShapeDtypeStruct((r, D_SSM), F32),
            jax.ShapeDtypeStruct((N_GROUP_BLOCKS, N_DIR, batch, STATE_LANES), F32),
        ],
        scratch_shapes=[
            pltpu.VMEM((rblk, STATE_LANES), F32),
            pltpu.VMEM((rblk, STATE_LANES), F32),
            pltpu.VMEM((N_DIR, batch, STATE_LANES), F32),
        ],
        compiler_params=_params("parallel"),
        name="s5_mixer",
    )(*args)


def _mixout_kernel(y_ref, gb_ref, zc_ref, zp_ref, zn_ref, x_ref, mod_ref, wglu_ref, wout_ref,
                   cw_ref, cb_ref, g_ref, x1_ref, h2_ref, *, batch, seg):
    tm = x_ref.shape[0]
    tl = tm // batch
    l0 = pl.program_id(0) * tl
    prev_ok = (l0 % seg != 0).astype(F32)
    next_ok = ((l0 + tl) % seg != 0).astype(F32)

    y = jax.nn.gelu(y_ref[...])
    y_ssm = y * jax.nn.sigmoid(jnp.dot(y.astype(BF16), wglu_ref[0], preferred_element_type=F32))

    zc = zc_ref[...]
    prev = jnp.concatenate([zp_ref[...] * prev_ok, zc[:tm - batch]], axis=0)
    nxt = jnp.concatenate([zc[batch:], zn_ref[...] * next_ok], axis=0)
    conv = prev * cw_ref[0, 0:1, :] + zc * cw_ref[0, 1:2, :] + nxt * cw_ref[0, 2:3, :] + cb_ref[0]
    y_conv = gb_ref[...] * conv

    out = jnp.dot(y_ssm.astype(BF16), wout_ref[0, :D_SSM, :], preferred_element_type=F32)
    out += jnp.dot(y_conv.astype(BF16), wout_ref[0, D_SSM:, :], preferred_element_type=F32)
    x1 = (_rows8(x_ref[...]) + mod_ref[0, 2][None] * _rows8(out)).reshape(tm, D_MODEL)
    x1_ref[...] = x1
    h2_ref[...] = _norm_mod(x1, g_ref[0], mod_ref[0, 4], mod_ref[0, 3]).astype(BF16)


def _mixout(y, gb, zc, x, mod, w_glu, w_out, conv_w, conv_b, g_ffn, batch, seg, layer):
    r = x.shape[0]
    tm = TM_MIXOUT
    assert seg % (tm // batch) == 0
    hb = tm // batch
    n_hb = r // batch
    row = lambda i: (i, 0)
    lay = lambda i: (layer, 0, 0)
    return pl.pallas_call(
        functools.partial(_mixout_kernel, batch=batch, seg=seg),
        grid=(r // tm,),
        in_specs=[
            pl.BlockSpec((tm, D_SSM), row),
            pl.BlockSpec((tm, D_CONV), row),
            pl.BlockSpec((tm, D_CONV), row),
            pl.BlockSpec((batch, D_CONV), lambda i: (jnp.maximum(i * hb - 1, 0), 0)),
            pl.BlockSpec((batch, D_CONV), lambda i: (jnp.minimum((i + 1) * hb, n_hb - 1), 0)),
            pl.BlockSpec((tm, D_MODEL), row),
            _resident((1, N_MOD, SUBLANES, D_MODEL), lambda i: (layer, 0, 0, 0)),
            _resident((1, D_SSM, D_SSM), lay),
            _resident((1, D_MODEL, D_MODEL), lay),
            _resident((1, 3, D_CONV), lay),
            _resident((1, 1, D_CONV), lay),
            _resident((1, 1, D_MODEL), lay),
        ],
        out_specs=[pl.BlockSpec((tm, D_MODEL), row), pl.BlockSpec((tm, D_MODEL), row)],
        out_shape=[jax.ShapeDtypeStruct((r, D_MODEL), F32), jax.ShapeDtypeStruct((r, D_MODEL), BF16)],
        compiler_params=_params("parallel"),
        name="mixer_out",
    )(y, gb, zc, zc, zc, x, mod, w_glu, w_out, conv_w, conv_b, g_ffn)


def _ffn_kernel(h_ref, x_ref, mod_ref, wg_ref, wu_ref, wd_ref, gf_ref, o_ref, acc, *, final):
    k = pl.program_id(1)

    @pl.when(k == 0)
    def _():
        acc[...] = jnp.zeros_like(acc)

    h = h_ref[...]
    gate = jnp.dot(h, wg_ref[0, 0], preferred_element_type=F32)
    up = jnp.dot(h, wu_ref[0, 0], preferred_element_type=F32)
    act = (jax.nn.silu(gate) * up).astype(BF16)
    acc[...] += jnp.dot(act, wd_ref[0], preferred_element_type=F32)

    @pl.when(k == pl.num_programs(1) - 1)
    def _():
        x2 = (_rows8(x_ref[...]) + mod_ref[0, 5][None] * _rows8(acc[...])).reshape(x_ref.shape)
        if final:
            ms = jnp.mean(x2 * x2, axis=-1, keepdims=True)
            x2 = x2 * lax.rsqrt(ms + EPS) * gf_ref[...]
            b = o_ref.shape[0]
            for t in range(o_ref.shape[1]):
                o_ref[:, t, :] = x2[t * b:(t + 1) * b, :]
        else:
            o_ref[...] = x2


def _ffn(h2, x1, mod, w_gate, w_up, w_down, g_final, layer, batch, final):
    r = x1.shape[0]
    tm, tf = TM_FFN, TF_FFN
    if final:
        out_spec = pl.BlockSpec((batch, tm // batch, D_MODEL), lambda i, k: (0, i, 0))
        out_shape = jax.ShapeDtypeStruct((batch, r // batch, D_MODEL), F32)
    else:
        out_spec = pl.BlockSpec((tm, D_MODEL), lambda i, k: (i, 0))
        out_shape = jax.ShapeDtypeStruct((r, D_MODEL), F32)
    return pl.pallas_call(
        functools.partial(_ffn_kernel, final=final),
        grid=(r // tm, D_FF // tf),
        in_specs=[
            pl.BlockSpec((tm, D_MODEL), lambda i, k: (i, 0)),
            pl.BlockSpec((tm, D_MODEL), lambda i, k: (i, 0)),
            pl.BlockSpec((1, N_MOD, SUBLANES, D_MODEL), lambda i, k: (layer, 0, 0, 0)),
            pl.BlockSpec((1, 1, D_MODEL, tf), lambda i, k: (layer, k, 0, 0)),
            pl.BlockSpec((1, 1, D_MODEL, tf), lambda i, k: (layer, k, 0, 0)),
            pl.BlockSpec((1, tf, D_MODEL), lambda i, k: (layer, k, 0)),
            pl.BlockSpec((1, D_MODEL), lambda i, k: (0, 0)),
        ],
        out_specs=out_spec,
        out_shape=out_shape,
        scratch_shapes=[pltpu.VMEM((tm, D_MODEL), F32)],
        compiler_params=_params("parallel", "arbitrary"),
        name="ffn",
    )(h2, x1, mod, w_gate, w_up, w_down, g_final)


def _s5_build_kernel(sb_ref, sc_ref, kt_ref, bc_ref, cct_ref, toep_ref):
    row_group = lax.broadcasted_iota(jnp.int32, (LANES, LANES), 0) // SSM_H
    lane = lax.broadcasted_iota(jnp.int32, (LANES, LANES), 1)
    dup_state = (lax.broadcasted_iota(jnp.int32, (SSM_STATE, LANES), 1) % SSM_STATE
                 == lax.broadcasted_iota(jnp.int32, (SSM_STATE, LANES), 0)).astype(BF16)
    dup_h = (lax.broadcasted_iota(jnp.int32, (SSM_H, LANES), 1) % SSM_H
             == lax.broadcasted_iota(jnp.int32, (SSM_H, LANES), 0)).astype(BF16)
    keep = [(row_group // 2 == q) & (lane // SSM_STATE == row_group % 2) for q in range(PAIRS)]
    for src_ref, dst_ref in ((sb_ref, bc_ref), (sc_ref, cct_ref)):
        for d in range(N_DIR):
            for i in range(T_BLK):
                for r in range(2):
                    wide = jnp.dot(src_ref[0, d, i, r].astype(BF16), dup_state, preferred_element_type=F32)
                    for q in range(PAIRS):
                        c0 = d * STATE_LANES + r * (STATE_LANES // 2) + q * LANES
                        dst_ref[0, i * LANES:(i + 1) * LANES, c0:c0 + LANES] = (
                            jnp.where(keep[q], wide, 0.0).astype(BF16))
    diag = row_group == lane // SSM_H
    for i_in in range(T_BLK):
        for i_out in range(T_BLK):
            wide = jnp.dot(kt_ref[0, i_in, i_out].astype(BF16), dup_h, preferred_element_type=F32)
            toep_ref[0, i_in * LANES:(i_in + 1) * LANES, i_out * LANES:(i_out + 1) * LANES] = (
                jnp.where(diag, wide, 0.0).astype(BF16))


def _cpow(lr, li, dt, k):
    m = jnp.exp(lr * dt * k)
    return m * jnp.cos(li * dt * k), m * jnp.sin(li * dt * k)


def _s5_matrices(lam_re, lam_im, log_dt, b_re, b_im, c_re, c_im):
    hi = lax.Precision.HIGHEST
    t = T_BLK
    nb = N_GROUP_BLOCKS
    lr = jnp.minimum(lam_re, LAM_RE_MAX)
    li = lam_im
    dt = jnp.exp(log_dt)[..., None]
    ar, ai = _cpow(lr, li, dt, 1.0)
    den = lr * lr + li * li
    qr = (((ar - 1.0) * lr + ai * li) / den)[:, :, None, :]
    qi = ((ai * lr - (ar - 1.0) * li) / den)[:, :, None, :]
    b_re_t = b_re.transpose(0, 1, 3, 2)
    b_im_t = b_im.transpose(0, 1, 3, 2)
    bbr = qr * b_re_t - qi * b_im_t
    bbi = qr * b_im_t + qi * b_re_t

    steps = jnp.arange(t, dtype=F32)
    e_in = jnp.stack([t - 1.0 - steps, steps], axis=0)[:, :, None, None]
    e_out = jnp.stack([steps + 1.0, t - steps], axis=0)[:, :, None, None]
    e_lag = jnp.stack([steps, steps], axis=0)[:, :, None, None]
    lr4, li4, dt4 = lr[:, None], li[:, None], dt[:, None]

    def times_pow(xr, xi, e):
        pr, pi = _cpow(lr4, li4, dt4, e)
        pr, pi = pr[:, :, :, None, :], pi[:, :, :, None, :]
        return xr[:, None] * pr - xi[:, None] * pi, xr[:, None] * pi + xi[:, None] * pr

    def per_block(re, im):
        s = jnp.stack([re, im], axis=2).reshape(N_DIR, t, 2, nb, LANES, SSM_STATE)
        return s.transpose(3, 0, 1, 2, 4, 5)

    cr, ci = times_pow(bbr, bbi, e_in)
    kr, ki = times_pow(c_re, c_im, e_out)
    lgr, lgi = times_pow(c_re, c_im, e_lag)
    klag = (jnp.einsum("dlgop,dghp->dlgho", lgr, bbr, precision=hi)
            - jnp.einsum("dlgop,dghp->dlgho", lgi, bbi, precision=hi))
    idx = jnp.arange(t)
    lag = idx[None, :] - idx[:, None]
    m5 = lambda m: m[:, :, None, None, None]
    kt = (jnp.where(m5(lag >= 0), klag[0][jnp.clip(lag, 0, t - 1)], 0.0)
          + jnp.where(m5(lag <= 0), klag[1][jnp.clip(-lag, 0, t - 1)], 0.0))
    kt = kt.reshape(t, t, nb, LANES, SSM_H).transpose(2, 0, 1, 3, 4)

    src_spec = pl.BlockSpec((1, N_DIR, t, 2, LANES, SSM_STATE), lambda g: (g, 0, 0, 0, 0, 0))
    wide_spec = pl.BlockSpec((1, t * LANES, N_DIR * STATE_LANES), lambda g: (g, 0, 0))
    wide_shape = jax.ShapeDtypeStruct((nb, t * LANES, N_DIR * STATE_LANES), BF16)
    bc, cct, toep = pl.pallas_call(
        _s5_build_kernel,
        grid=(nb,),
        in_specs=[src_spec, src_spec, pl.BlockSpec((1, t, t, LANES, SSM_H), lambda g: (g, 0, 0, 0, 0))],
        out_specs=[wide_spec, wide_spec, pl.BlockSpec((1, t * LANES, t * LANES), lambda g: (g, 0, 0))],
        out_shape=[wide_shape, wide_shape, jax.ShapeDtypeStruct((nb, t * LANES, t * LANES), BF16)],
        compiler_params=_params("parallel"),
        name="s5_build",
    )(per_block(cr, ci), per_block(kr, -ki), kt)

    atr, ati = _cpow(lr, li, dt, float(t))
    a = jnp.stack([atr, ati], axis=0)
    a = a.reshape(2, N_DIR, nb, PAIRS * LANES).transpose(2, 0, 1, 3).reshape(nb, 2, 1, N_DIR * PAIRS * LANES)
    a = jnp.broadcast_to(a, (nb, 2, SUBLANES, N_DIR * PAIRS * LANES))
    return bc, cct, toep, a


def _init_states(st):
    b = st.shape[0]
    s = st.reshape(b, N_DIR, 2, N_GROUP_BLOCKS, STATE_LANES // 2)
    return s.transpose(3, 1, 0, 2, 4).reshape(N_GROUP_BLOCKS, N_DIR, b, STATE_LANES)


def _final_states(fin):
    b = fin.shape[2]
    s = fin.reshape(N_GROUP_BLOCKS, N_DIR, b, 2, STATE_LANES // 2)
    return s.transpose(2, 1, 3, 0, 4).reshape(b, N_DIR, 2, SSM_GROUPS, SSM_STATE)


def kernel(x_prompt, x_sample, state_ssm, c, c_ctx, w_ada, b_ada, g_mix, w_in, ssm_lam_re, ssm_lam_im, ssm_log_dt, ssm_b_re, ssm_b_im, ssm_c_re, ssm_c_im, ssm_d, w_glu, conv_w, conv_b, w_out, g_ffn, w_gate, w_up, w_down, g_final):
    n_ctx, l_ctx, _ = x_prompt.shape
    n_dec, l_dec, _ = x_sample.shape

    cond = jnp.zeros((16, D_MODEL), F32).at[0].set(c_ctx).at[1:1 + n_dec].set(c)
    mod = _modulation(cond, w_ada, b_ada).reshape(DEPTH, 16, N_MOD, D_MODEL)
    mod_ctx = jnp.broadcast_to(mod[:, 0, :, None, :], (DEPTH, N_MOD, SUBLANES, D_MODEL))
    mod_dec = mod[:, 1:1 + n_dec].transpose(0, 2, 1, 3)

    w_in_t = _cast_col_tiles(w_in, TN_INPROJ, parts=4)
    w_glu_b = _cast_rows(w_glu, 512)
    w_out_b = _cast_rows(w_out, 512)
    w_gate_b = _cast_col_tiles(w_gate, TF_FFN)
    w_up_b = _cast_col_tiles(w_up, TF_FFN)
    w_down_b = _cast_rows(w_down, TF_FFN)
    g_mix3 = g_mix[:, None, :]
    g_ffn3 = g_ffn[:, None, :]
    conv_b3 = conv_b[:, None, :]
    ssm_d3 = ssm_d[:, None, :]

    groups = [
        dict(x=x_prompt, batch=n_ctx, seg=l_ctx, mod=mod_ctx, s0=None),
        dict(x=x_sample, batch=n_dec, seg=GRID_W, mod=mod_dec, s0=state_ssm.astype(F32)),
    ]

    finals = []
    for l in range(DEPTH):
        bc, cct, toep, a = _s5_matrices(ssm_lam_re[l], ssm_lam_im[l], ssm_log_dt[l], ssm_b_re[l], ssm_b_im[l],
                                 ssm_c_re[l], ssm_c_im[l])
        for grp in groups:
            batch = grp["batch"]
            s0 = None if grp["s0"] is None else _init_states(grp["s0"][:, l])
            if l == 0:
                x, u, gb, zc = _inproj(grp["x"], grp["mod"], g_mix3, w_in_t, l, batch_major=True)
            else:
                x = grp["x"]
                u, gb, zc = _inproj(x, grp["mod"], g_mix3, w_in_t, l, batch_major=False)
            y, fin = _s5(u, bc, cct, toep, a, ssm_d3, s0, batch, l)
            x1, h2 = _mixout(y, gb, zc, x, grp["mod"], w_glu_b, w_out_b, conv_w, conv_b3, g_ffn3,
                             batch, grp["seg"], l)
            grp["x"] = _ffn(h2, x1, grp["mod"], w_gate_b, w_up_b, w_down_b, g_final[None], l, batch,
                            final=(l == DEPTH - 1))
            if grp["s0"] is None:
                finals.append(_final_states(fin))

    new_state = jnp.stack(finals, axis=1).astype(x_prompt.dtype)
    return (groups[0]["x"], groups[1]["x"], new_state)
```

```python
import functools

import jax
import jax.numpy as jnp
from jax import lax
from jax.experimental import pallas as pl
from jax.experimental.pallas import tpu as pltpu

D_MODEL = 2048
DEPTH = 2
GRID_W = 64
D_SSM = 1024
D_CONV = 1024
SSM_H = 16
SSM_GROUPS = 64
SSM_STATE = 64
N_DIR = 2
D_FF = 5632
N_MOD = 6
EPS = 1e-6
LAM_RE_MAX = -1e-4

SUBLANES = 8
LANES = 128
GROUPS_PER_BLOCK = LANES // SSM_H
N_GROUP_BLOCKS = SSM_GROUPS // GROUPS_PER_BLOCK
PAIRS = GROUPS_PER_BLOCK // 2
STATE_LANES = GROUPS_PER_BLOCK * 2 * SSM_STATE
SCAN_STEPS = 64
T_BLK = 4
VMEM_LIMIT = 56 * 1024 * 1024

TM_INPROJ = 1024
TM_MIXOUT = 512
TM_FFN = 512
TN_INPROJ = 1024
TF_FFN = 512

F32 = jnp.float32
BF16 = jnp.bfloat16


def _params(*sem):
    return pltpu.CompilerParams(dimension_semantics=sem, vmem_limit_bytes=VMEM_LIMIT)


def _resident(shape, index_map):
    return pl.BlockSpec(shape, index_map, pipeline_mode=pl.Buffered(1))


def _rows8(a):
    return a.reshape(a.shape[0] // SUBLANES, SUBLANES, a.shape[1])


def _norm_mod(x, gain, scale, shift):
    ms = jnp.mean(x * x, axis=-1, keepdims=True)
    y = x * lax.rsqrt(ms + EPS) * gain
    h = _rows8(y) * (1.0 + scale)[None] + shift[None]
    return h.reshape(x.shape)


def _cast_kernel(w_ref, o_ref):
    o_ref[...] = w_ref[...].astype(BF16).reshape(o_ref.shape)


def _cast_col_tiles(w, tn, parts=1):
    _, k, n = w.shape
    nt = n // tn
    q = tn // parts
    return pl.pallas_call(
        _cast_kernel,
        grid=(DEPTH, nt, parts),
        in_specs=[pl.BlockSpec((1, k, q), lambda l, j, p: (l, 0, p * nt + j))],
        out_specs=pl.BlockSpec((1, 1, k, q), lambda l, j, p: (l, j, 0, p)),
        out_shape=jax.ShapeDtypeStruct((DEPTH, nt, k, tn), BF16),
        compiler_params=_params("parallel", "parallel", "parallel"),
        name="cast_col_tiles",
    )(w)


def _cast_pair_kernel(a_ref, b_ref, o_ref):
    tn = a_ref.shape[2]
    o_ref[0, 0, :, :tn] = a_ref[0].astype(BF16)
    o_ref[0, 0, :, tn:] = b_ref[0].astype(BF16)


def _cast_pair_col_tiles(wa, wb, tn):
    _, k, n = wa.shape
    spec = pl.BlockSpec((1, k, tn), lambda l, j: (l, 0, j))
    return pl.pallas_call(
        _cast_pair_kernel,
        grid=(DEPTH, n // tn),
        in_specs=[spec, spec],
        out_specs=pl.BlockSpec((1, 1, k, 2 * tn), lambda l, j: (l, j, 0, 0)),
        out_shape=jax.ShapeDtypeStruct((DEPTH, n // tn, k, 2 * tn), BF16),
        compiler_params=_params("parallel", "parallel"),
        name="cast_pair_col_tiles",
    )(wa, wb)


def _cast_rows(w, tk):
    _, k, n = w.shape
    return pl.pallas_call(
        _cast_kernel,
        grid=(DEPTH, k // tk),
        in_specs=[pl.BlockSpec((1, tk, n), lambda l, i: (l, i, 0))],
        out_specs=pl.BlockSpec((1, tk, n), lambda l, i: (l, i, 0)),
        out_shape=jax.ShapeDtypeStruct((DEPTH, k, n), BF16),
        compiler_params=_params("parallel", "parallel"),
        name="cast_rows",
    )(w)


def _mod_kernel(cond_ref, w_ref, b_ref, o_ref):
    s = jax.nn.silu(cond_ref[...]).astype(BF16)
    o_ref[0] = jnp.dot(s, w_ref[0].astype(BF16), preferred_element_type=F32) + b_ref[0]


def _modulation(cond, w_ada, b_ada):
    tn = 1024
    n = N_MOD * D_MODEL
    return pl.pallas_call(
        _mod_kernel,
        grid=(DEPTH, n // tn),
        in_specs=[
            pl.BlockSpec((16, D_MODEL), lambda l, j: (0, 0)),
            pl.BlockSpec((1, D_MODEL, tn), lambda l, j: (l, 0, j)),
            pl.BlockSpec((1, 1, tn), lambda l, j: (l, 0, j)),
        ],
        out_specs=pl.BlockSpec((1, 16, tn), lambda l, j: (l, 0, j)),
        out_shape=jax.ShapeDtypeStruct((DEPTH, 16, n), F32),
        compiler_params=_params("parallel", "parallel"),
        name="adaln_modulation",
    )(cond, w_ada, b_ada.reshape(DEPTH, 1, n))


def _inproj_kernel(x_ref, mod_ref, g_ref, w_ref, *out_refs, batch_major):
    if batch_major:
        xt_ref, u_ref, gb_ref, zc_ref, h_scr = out_refs
    else:
        u_ref, gb_ref, zc_ref, h_scr = out_refs

    @pl.when(pl.program_id(1) == 0)
    def _():
        if batch_major:
            x = jnp.concatenate([x_ref[:, t, :] for t in range(x_ref.shape[1])], axis=0)
            xt_ref[...] = x
        else:
            x = x_ref[...]
        h = _norm_mod(x, g_ref[0], mod_ref[0, 1], mod_ref[0, 0])
        h_scr[...] = h.astype(BF16)

    z = jnp.dot(h_scr[...], w_ref[0, 0], preferred_element_type=F32)
    q = z.shape[1] // 4
    u_ref[...] = z[:, :q]
    gb_ref[...] = z[:, q:2 * q]
    zc_ref[...] = z[:, 2 * q:3 * q] * z[:, 3 * q:]


def _inproj_ahead_kernel(x0_ref, x_ref, mod_ref, g_ref, w_ref, u_ref, gb_ref, zc_ref, h_even, h_odd):
    i = pl.program_id(0)
    j = pl.program_id(1)
    tq = h_even.shape[0] // pl.num_programs(1)
    norm = lambda x: _norm_mod(x, g_ref[0], mod_ref[0, 1], mod_ref[0, 0]).astype(BF16)

    @pl.when((i == 0) & (j == 0))
    def _():
        h_even[...] = norm(x0_ref[...])

    def step(h_cur, h_next):
        rows = pl.ds(pl.multiple_of(j * tq, tq), tq)
        h_next[rows, :] = norm(x_ref[rows, :])
        z = jnp.dot(h_cur[...], w_ref[0, 0], preferred_element_type=F32)
        q = z.shape[1] // 4
        u_ref[...] = z[:, :q]
        gb_ref[...] = z[:, q:2 * q]
        zc_ref[...] = z[:, 2 * q:3 * q] * z[:, 3 * q:]

    pl.when(i % 2 == 0)(lambda: step(h_even, h_odd))
    pl.when(i % 2 == 1)(lambda: step(h_odd, h_even))


def _inproj_ahead(x, mod, g_mix, w_in_tiled, layer):
    tm, tn = TM_INPROJ, TN_INPROJ
    q = tn // 4
    r = x.shape[0]
    n = r // tm
    out = jax.ShapeDtypeStruct((r, D_SSM), F32)
    return pl.pallas_call(
        _inproj_ahead_kernel,
        grid=(n, 4 * D_SSM // tn),
        in_specs=[
            _resident((tm, D_MODEL), lambda i, j: (0, 0)),
            pl.BlockSpec((tm, D_MODEL), lambda i, j: (jnp.minimum(i + 1, n - 1), 0)),
            pl.BlockSpec((1, N_MOD, SUBLANES, D_MODEL), lambda i, j: (layer, 0, 0, 0)),
            pl.BlockSpec((1, 1, D_MODEL), lambda i, j: (layer, 0, 0)),
            pl.BlockSpec((1, 1, D_MODEL, tn), lambda i, j: (layer, j, 0, 0)),
        ],
        out_specs=[pl.BlockSpec((tm, q), lambda i, j: (i, j))] * 3,
        out_shape=[out, out, out],
        scratch_shapes=[pltpu.VMEM((tm, D_MODEL), BF16), pltpu.VMEM((tm, D_MODEL), BF16)],
        compiler_params=_params("arbitrary", "arbitrary"),
        name="inproj_ahead",
    )(x, x, mod, g_mix, w_in_tiled)


def _inproj(x, mod, g_mix, w_in_tiled, layer, batch_major):
    tn = TN_INPROJ
    tm = TM_INPROJ // 2 if batch_major else TM_INPROJ
    q = tn // 4
    if batch_major:
        b, l, _ = x.shape
        r = b * l
        x_spec = pl.BlockSpec((b, tm // b, D_MODEL), lambda i, j: (0, i, 0))
    else:
        r = x.shape[0]
        x_spec = pl.BlockSpec((tm, D_MODEL), lambda i, j: (i, 0))
    out = jax.ShapeDtypeStruct((r, D_SSM), F32)
    out_specs = [pl.BlockSpec((tm, q), lambda i, j: (i, j))] * 3
    out_shape = [out, out, out]
    if batch_major:
        out_specs = [pl.BlockSpec((tm, D_MODEL), lambda i, j: (i, 0))] + out_specs
        out_shape = [jax.ShapeDtypeStruct((r, D_MODEL), F32)] + out_shape
    return pl.pallas_call(
        functools.partial(_inproj_kernel, batch_major=batch_major),
        grid=(r // tm, 4 * D_SSM // tn),
        in_specs=[
            x_spec,
            pl.BlockSpec((1, N_MOD, SUBLANES, D_MODEL), lambda i, j: (layer, 0, 0, 0)),
            pl.BlockSpec((1, 1, D_MODEL), lambda i, j: (layer, 0, 0)),
            pl.BlockSpec((1, 1, D_MODEL, tn), lambda i, j: (layer, j, 0, 0)),
        ],
        out_specs=out_specs,
        out_shape=out_shape,
        scratch_shapes=[pltpu.VMEM((tm, D_MODEL), BF16)],
        compiler_params=_params("parallel", "arbitrary"),
        name="inproj",
    )(x, mod, g_mix, w_in_tiled)


def _s5_kernel(*refs, batch, has_init):
    if has_init:
        u_ref, bc_ref, cc_ref, toep_ref, a_ref, d_ref, s0_ref, y_ref, fin_ref, sf, sb, xs = refs
    else:
        u_ref, bc_ref, cc_ref, toep_ref, a_ref, d_ref, y_ref, fin_ref, sf, sb, xs = refs
        s0_ref = None
    rows = u_ref.shape[0]
    rc = SCAN_STEPS * batch
    n_chunks = rows // rc
    nblk = SCAN_STEPS // T_BLK
    rblk = nblk * batch
    halves = batch // SUBLANES
    tl = T_BLK * LANES

    y_ref[...] = d_ref[0] * u_ref[...]
    if has_init:
        xs[...] = s0_ref[0]
    else:
        xs[...] = jnp.zeros_like(xs)

    def gather(r0):
        uc = u_ref[pl.ds(r0, rc), :].reshape(nblk, T_BLK, batch, LANES)
        return jnp.concatenate([uc[:, i].reshape(rblk, LANES) for i in range(T_BLK)], axis=1).astype(BF16)

    def scatter_add(r0, yb):
        parts = [yb[:, i * LANES:(i + 1) * LANES].reshape(nblk, 1, batch, LANES) for i in range(T_BLK)]
        y_ref[pl.ds(r0, rc), :] += jnp.concatenate(parts, axis=1).reshape(rc, LANES)

    def chunk(ci, carry):
        rf = pl.multiple_of(ci * rc, rc)
        rb = pl.multiple_of((n_chunks - 1 - ci) * rc, rc)
        lf = gather(rf)
        lb = gather(rb)
        sf[...] = jnp.dot(lf, bc_ref[0, :, :STATE_LANES], preferred_element_type=F32)
        sb[...] = jnp.dot(lb, bc_ref[0, :, STATE_LANES:], preferred_element_type=F32)
        for half in range(halves):
            hs = slice(half * SUBLANES, (half + 1) * SUBLANES)
            for pair in range(PAIRS):
                re = slice(pair * LANES, (pair + 1) * LANES)
                im = slice(STATE_LANES // 2 + pair * LANES, STATE_LANES // 2 + (pair + 1) * LANES)
                for d, s in ((0, sf), (1, sb)):
                    al = slice(d * PAIRS * LANES + pair * LANES, d * PAIRS * LANES + (pair + 1) * LANES)
                    ar = a_ref[0, 0, :, al]
                    ai = a_ref[0, 1, :, al]
                    xr = xs[d, hs, re]
                    xi = xs[d, hs, im]
                    for t in range(nblk):
                        blk = t if d == 0 else nblk - 1 - t
                        rs = slice(blk * batch + half * SUBLANES, blk * batch + (half + 1) * SUBLANES)
                        sr = s[rs, re]
                        si = s[rs, im]
                        s[rs, re] = xr
                        s[rs, im] = xi
                        xr, xi = ar * xr - ai * xi + sr, ar * xi + ai * xr + si
                    xs[d, hs, re] = xr
                    xs[d, hs, im] = xi
        yf = jnp.dot(sf[...].astype(BF16), cc_ref[0, :STATE_LANES, :], preferred_element_type=F32)
        yf += jnp.dot(lf, toep_ref[0], preferred_element_type=F32)
        yb = jnp.dot(sb[...].astype(BF16), cc_ref[0, STATE_LANES:, :], preferred_element_type=F32)
        scatter_add(rf, yf)
        scatter_add(rb, yb)
        return carry

    lax.fori_loop(0, n_chunks, chunk, 0)
    fin_ref[0] = xs[...]


def _s5(u, bc, cc, toep, a, d_skip, s0, batch, layer):
    r = u.shape[0]
    rblk = (SCAN_STEPS // T_BLK) * batch
    tl = T_BLK * LANES
    has_init = s0 is not None
    in_specs = [
        pl.BlockSpec((r, LANES), lambda g: (0, g)),
        pl.BlockSpec((1, tl, N_DIR * STATE_LANES), lambda g: (g, 0, 0)),
        pl.BlockSpec((1, N_DIR * STATE_LANES, tl), lambda g: (g, 0, 0)),
        pl.BlockSpec((1, tl, tl), lambda g: (g, 0, 0)),
        pl.BlockSpec((1, 2, SUBLANES, N_DIR * PAIRS * LANES), lambda g: (g, 0, 0, 0)),
        pl.BlockSpec((1, 1, LANES), lambda g: (layer, 0, g)),
    ]
    args = [u, bc, cc, toep, a, d_skip]
    if has_init:
        in_specs.append(pl.BlockSpec((1, N_DIR, batch, STATE_LANES), lambda g: (g, 0, 0, 0)))
        args.append(s0)
    return pl.pallas_call(
        functools.partial(_s5_kernel, batch=batch, has_init=has_init),
        grid=(N_GROUP_BLOCKS,),
        in_specs=in_specs,
        out_specs=[
            pl.BlockSpec((r, LANES), lambda g: (0, g)),
            pl.BlockSpec((1, N_DIR, batch, STATE_LANES), lambda g: (g, 0, 0, 0)),
        ],
        out_shape=[
            jax.ShapeDtypeStruct((r, D_SSM), F32),
            jax.ShapeDtypeStruct((N_GROUP_BLOCKS, N_DIR, batch, STATE_LANES), F32),
        ],
        scratch_shapes=[
            pltpu.VMEM((rblk, STATE_LANES), F32),
            pltpu.VMEM((rblk, STATE_LANES), F32),
            pltpu.VMEM((N_DIR, batch, STATE_LANES), F32),
        ],
        compiler_params=_params("parallel"),
        name="s5_mixer",
    )(*args)


def _mixout_kernel(y_ref, gb_ref, zc_ref, zp_ref, zn_ref, x_ref, mod_ref, wglu_ref, wout_ref,
                   cw_ref, cb_ref, g_ref, x1_ref, h2_ref, *, batch, seg):
    tm = x_ref.shape[0]
    tl = tm // batch
    l0 = pl.program_id(0) * tl
    prev_ok = (l0 % seg != 0).astype(F32)
    next_ok = ((l0 + tl) % seg != 0).astype(F32)

    y = jax.nn.gelu(y_ref[...])
    y_ssm = y * jax.nn.sigmoid(jnp.dot(y.astype(BF16), wglu_ref[0], preferred_element_type=F32))

    zc = zc_ref[...]
    prev = jnp.concatenate([zp_ref[...] * prev_ok, zc[:tm - batch]], axis=0)
    nxt = jnp.concatenate([zc[batch:], zn_ref[...] * next_ok], axis=0)
    conv = prev * cw_ref[0, 0:1, :] + zc * cw_ref[0, 1:2, :] + nxt * cw_ref[0, 2:3, :] + cb_ref[0]
    y_conv = gb_ref[...] * conv

    out = jnp.dot(y_ssm.astype(BF16), wout_ref[0, :D_SSM, :], preferred_element_type=F32)
    out += jnp.dot(y_conv.astype(BF16), wout_ref[0, D_SSM:, :], preferred_element_type=F32)
    x1 = (_rows8(x_ref[...]) + mod_ref[0, 2][None] * _rows8(out)).reshape(tm, D_MODEL)
    x1_ref[...] = x1
    h2_ref[...] = _norm_mod(x1, g_ref[0], mod_ref[0, 4], mod_ref[0, 3]).astype(BF16)


def _mixout(y, gb, zc, x, mod, w_glu, w_out, conv_w, conv_b, g_ffn, batch, seg, layer):
    r = x.shape[0]
    tm = TM_MIXOUT
    assert seg % (tm // batch) == 0
    hb = tm // batch
    n_hb = r // batch
    row = lambda i: (i, 0)
    lay = lambda i: (layer, 0, 0)
    return pl.pallas_call(
        functools.partial(_mixout_kernel, batch=batch, seg=seg),
        grid=(r // tm,),
        in_specs=[
            pl.BlockSpec((tm, D_SSM), row),
            pl.BlockSpec((tm, D_CONV), row),
            pl.BlockSpec((tm, D_CONV), row),
            pl.BlockSpec((batch, D_CONV), lambda i: (jnp.maximum(i * hb - 1, 0), 0)),
            pl.BlockSpec((batch, D_CONV), lambda i: (jnp.minimum((i + 1) * hb, n_hb - 1), 0)),
            pl.BlockSpec((tm, D_MODEL), row),
            _resident((1, N_MOD, SUBLANES, D_MODEL), lambda i: (layer, 0, 0, 0)),
            _resident((1, D_SSM, D_SSM), lay),
            _resident((1, D_MODEL, D_MODEL), lay),
            _resident((1, 3, D_CONV), lay),
            _resident((1, 1, D_CONV), lay),
            _resident((1, 1, D_MODEL), lay),
        ],
        out_specs=[pl.BlockSpec((tm, D_MODEL), row), pl.BlockSpec((tm, D_MODEL), row)],
        out_shape=[jax.ShapeDtypeStruct((r, D_MODEL), F32), jax.ShapeDtypeStruct((r, D_MODEL), BF16)],
        compiler_params=_params("parallel"),
        name="mixer_out",
    )(y, gb, zc, zc, zc, x, mod, w_glu, w_out, conv_w, conv_b, g_ffn)


def _ffn_kernel(h_ref, x_ref, mod_ref, wgu_ref, wd_ref, gf_ref, o_ref, acc, *, final):
    k = pl.program_id(1)

    @pl.when(k == 0)
    def _():
        acc[...] = jnp.zeros_like(acc)

    gate_up = jnp.dot(h_ref[...], wgu_ref[0, 0], preferred_element_type=F32)
    tf = gate_up.shape[1] // 2
    act = (jax.nn.silu(gate_up[:, :tf]) * gate_up[:, tf:]).astype(BF16)
    acc[...] += jnp.dot(act, wd_ref[0], preferred_element_type=F32)

    @pl.when(k == pl.num_programs(1) - 1)
    def _():
        x2 = (_rows8(x_ref[...]) + mod_ref[0, 5][None] * _rows8(acc[...])).reshape(x_ref.shape)
        if final:
            ms = jnp.mean(x2 * x2, axis=-1, keepdims=True)
            x2 = x2 * lax.rsqrt(ms + EPS) * gf_ref[...]
            b = o_ref.shape[0]
            for t in range(o_ref.shape[1]):
                o_ref[:, t, :] = x2[t * b:(t + 1) * b, :]
        else:
            o_ref[...] = x2


def _ffn(h2, x1, mod, w_gate_up, w_down, g_final, layer, batch, final):
    r = x1.shape[0]
    tm, tf = TM_FFN, TF_FFN
    if final:
        out_spec = pl.BlockSpec((batch, tm // batch, D_MODEL), lambda i, k: (0, i, 0))
        out_shape = jax.ShapeDtypeStruct((batch, r // batch, D_MODEL), F32)
    else:
        out_spec = pl.BlockSpec((tm, D_MODEL), lambda i, k: (i, 0))
        out_shape = jax.ShapeDtypeStruct((r, D_MODEL), F32)
    return pl.pallas_call(
        functools.partial(_ffn_kernel, final=final),
        grid=(r // tm, D_FF // tf),
        in_specs=[
            pl.BlockSpec((tm, D_MODEL), lambda i, k: (i, 0)),
            pl.BlockSpec((tm, D_MODEL), lambda i, k: (i, 0)),
            pl.BlockSpec((1, N_MOD, SUBLANES, D_MODEL), lambda i, k: (layer, 0, 0, 0)),
            pl.BlockSpec((1, 1, D_MODEL, 2 * tf), lambda i, k: (layer, k, 0, 0)),
            pl.BlockSpec((1, tf, D_MODEL), lambda i, k: (layer, k, 0)),
            pl.BlockSpec((1, D_MODEL), lambda i, k: (0, 0)),
        ],
        out_specs=out_spec,
        out_shape=out_shape,
        scratch_shapes=[pltpu.VMEM((tm, D_MODEL), F32)],
        compiler_params=_params("parallel", "arbitrary"),
        name="ffn",
    )(h2, x1, mod, w_gate_up, w_down, g_final)


def _s5_build_kernel(sb_ref, sc_ref, kt_ref, bc_ref, cc_ref, toep_ref):
    row_group = lax.broadcasted_iota(jnp.int32, (LANES, LANES), 0) // SSM_H
    lane = lax.broadcasted_iota(jnp.int32, (LANES, LANES), 1)
    dup_state = (lax.broadcasted_iota(jnp.int32, (SSM_STATE, LANES), 1) % SSM_STATE
                 == lax.broadcasted_iota(jnp.int32, (SSM_STATE, LANES), 0)).astype(BF16)
    dup_h = (lax.broadcasted_iota(jnp.int32, (SSM_H, LANES), 1) % SSM_H
             == lax.broadcasted_iota(jnp.int32, (SSM_H, LANES), 0)).astype(BF16)
    keep = [(row_group // 2 == q) & (lane // SSM_STATE == row_group % 2) for q in range(PAIRS)]
    for d in range(N_DIR):
        for i in range(T_BLK):
            for r in range(2):
                wide_b = jnp.dot(sb_ref[0, d, i, r].astype(BF16), dup_state, preferred_element_type=F32)
                wide_c = jnp.dot(sc_ref[0, d, i, r].astype(BF16), dup_state, preferred_element_type=F32)
                for q in range(PAIRS):
                    c0 = d * STATE_LANES + r * (STATE_LANES // 2) + q * LANES
                    bc_ref[0, i * LANES:(i + 1) * LANES, c0:c0 + LANES] = (
                        jnp.where(keep[q], wide_b, 0.0).astype(BF16))
                    cc_ref[0, c0:c0 + LANES, i * LANES:(i + 1) * LANES] = (
                        jnp.where(keep[q], wide_c, 0.0).T.astype(BF16))
    diag = row_group == lane // SSM_H
    for i_in in range(T_BLK):
        for i_out in range(T_BLK):
            wide = jnp.dot(kt_ref[0, i_in, i_out].astype(BF16), dup_h, preferred_element_type=F32)
            toep_ref[0, i_in * LANES:(i_in + 1) * LANES, i_out * LANES:(i_out + 1) * LANES] = (
                jnp.where(diag, wide, 0.0).astype(BF16))


def _cpow(lr, li, dt, k):
    m = jnp.exp(lr * dt * k)
    return m * jnp.cos(li * dt * k), m * jnp.sin(li * dt * k)


def _s5_matrices(lam_re, lam_im, log_dt, b_re, b_im, c_re, c_im):
    hi = lax.Precision.HIGHEST
    t = T_BLK
    nb = N_GROUP_BLOCKS
    lr = jnp.minimum(lam_re, LAM_RE_MAX)
    li = lam_im
    dt = jnp.exp(log_dt)[..., None]
    ar, ai = _cpow(lr, li, dt, 1.0)
    den = lr * lr + li * li
    qr = (((ar - 1.0) * lr + ai * li) / den)[:, :, None, :]
    qi = ((ai * lr - (ar - 1.0) * li) / den)[:, :, None, :]
    b_re_t = b_re.transpose(0, 1, 3, 2)
    b_im_t = b_im.transpose(0, 1, 3, 2)
    bbr = qr * b_re_t - qi * b_im_t
    bbi = qr * b_im_t + qi * b_re_t

    steps = jnp.arange(t, dtype=F32)
    e_in = jnp.stack([t - 1.0 - steps, steps], axis=0)[:, :, None, None]
    e_out = jnp.stack([steps + 1.0, t - steps], axis=0)[:, :, None, None]
    e_lag = jnp.stack([steps, steps], axis=0)[:, :, None, None]
    lr4, li4, dt4 = lr[:, None], li[:, None], dt[:, None]

    def times_pow(xr, xi, e):
        pr, pi = _cpow(lr4, li4, dt4, e)
        pr, pi = pr[:, :, :, None, :], pi[:, :, :, None, :]
        return xr[:, None] * pr - xi[:, None] * pi, xr[:, None] * pi + xi[:, None] * pr

    def per_block(re, im):
        s = jnp.stack([re, im], axis=2).reshape(N_DIR, t, 2, nb, LANES, SSM_STATE)
        return s.transpose(3, 0, 1, 2, 4, 5)

    cr, ci = times_pow(bbr, bbi, e_in)
    kr, ki = times_pow(c_re, c_im, e_out)
    lgr, lgi = times_pow(c_re, c_im, e_lag)
    klag = (jnp.einsum("dlgop,dghp->dlgho", lgr, bbr, precision=hi)
            - jnp.einsum("dlgop,dghp->dlgho", lgi, bbi, precision=hi))
    idx = jnp.arange(t)
    lag = idx[None, :] - idx[:, None]
    m5 = lambda m: m[:, :, None, None, None]
    kt = (jnp.where(m5(lag >= 0), klag[0][jnp.clip(lag, 0, t - 1)], 0.0)
          + jnp.where(m5(lag <= 0), klag[1][jnp.clip(-lag, 0, t - 1)], 0.0))
    kt = kt.reshape(t, t, nb, LANES, SSM_H).transpose(2, 0, 1, 3, 4)

    src_spec = pl.BlockSpec((1, N_DIR, t, 2, LANES, SSM_STATE), lambda g: (g, 0, 0, 0, 0, 0))
    wide_spec = pl.BlockSpec((1, t * LANES, N_DIR * STATE_LANES), lambda g: (g, 0, 0))
    wide_shape = jax.ShapeDtypeStruct((nb, t * LANES, N_DIR * STATE_LANES), BF16)
    bc, cc, toep = pl.pallas_call(
        _s5_build_kernel,
        grid=(nb,),
        in_specs=[src_spec, src_spec, pl.BlockSpec((1, t, t, LANES, SSM_H), lambda g: (g, 0, 0, 0, 0))],
        out_specs=[wide_spec, pl.BlockSpec((1, N_DIR * STATE_LANES, t * LANES), lambda g: (g, 0, 0)),
                   pl.BlockSpec((1, t * LANES, t * LANES), lambda g: (g, 0, 0))],
        out_shape=[wide_shape, jax.ShapeDtypeStruct((nb, N_DIR * STATE_LANES, t * LANES), BF16),
                   jax.ShapeDtypeStruct((nb, t * LANES, t * LANES), BF16)],
        compiler_params=_params("parallel"),
        name="s5_build",
    )(per_block(cr, ci), per_block(kr, -ki), kt)

    atr, ati = _cpow(lr, li, dt, float(t))
    a = jnp.stack([atr, ati], axis=0)
    a = a.reshape(2, N_DIR, nb, PAIRS * LANES).transpose(2, 0, 1, 3).reshape(nb, 2, 1, N_DIR * PAIRS * LANES)
    a = jnp.broadcast_to(a, (nb, 2, SUBLANES, N_DIR * PAIRS * LANES))
    return bc, cc, toep, a


def _init_states(st):
    b = st.shape[0]
    s = st.reshape(b, N_DIR, 2, N_GROUP_BLOCKS, STATE_LANES // 2)
    return s.transpose(3, 1, 0, 2, 4).reshape(N_GROUP_BLOCKS, N_DIR, b, STATE_LANES)


def _final_states(fin):
    b = fin.shape[2]
    s = fin.reshape(N_GROUP_BLOCKS, N_DIR, b, 2, STATE_LANES // 2)
    return s.transpose(2, 1, 3, 0, 4).reshape(b, N_DIR, 2, SSM_GROUPS, SSM_STATE)


def kernel(x_prompt, x_sample, state_ssm, c, c_ctx, w_ada, b_ada, g_mix, w_in, ssm_lam_re, ssm_lam_im, ssm_log_dt, ssm_b_re, ssm_b_im, ssm_c_re, ssm_c_im, ssm_d, w_glu, conv_w, conv_b, w_out, g_ffn, w_gate, w_up, w_down, g_final):
    n_ctx, l_ctx, _ = x_prompt.shape
    n_dec, l_dec, _ = x_sample.shape

    cond = jnp.zeros((16, D_MODEL), F32).at[0].set(c_ctx).at[1:1 + n_dec].set(c)
    mod = _modulation(cond, w_ada, b_ada).reshape(DEPTH, 16, N_MOD, D_MODEL)
    mod_ctx = jnp.broadcast_to(mod[:, 0, :, None, :], (DEPTH, N_MOD, SUBLANES, D_MODEL))
    mod_dec = mod[:, 1:1 + n_dec].transpose(0, 2, 1, 3)

    w_in_t = _cast_col_tiles(w_in, TN_INPROJ, parts=4)
    w_glu_b = _cast_rows(w_glu, 512)
    w_out_b = _cast_rows(w_out, 512)
    w_gate_up_b = _cast_pair_col_tiles(w_gate, w_up, TF_FFN)
    w_down_b = _cast_rows(w_down, TF_FFN)
    g_mix3 = g_mix[:, None, :]
    g_ffn3 = g_ffn[:, None, :]
    conv_b3 = conv_b[:, None, :]
    ssm_d3 = ssm_d[:, None, :]

    groups = [
        dict(x=x_prompt, batch=n_ctx, seg=l_ctx, mod=mod_ctx, s0=None),
        dict(x=x_sample, batch=n_dec, seg=GRID_W, mod=mod_dec, s0=state_ssm.astype(F32)),
    ]

    finals = []
    for l in range(DEPTH):
        bc, cc, toep, a = _s5_matrices(ssm_lam_re[l], ssm_lam_im[l], ssm_log_dt[l], ssm_b_re[l], ssm_b_im[l],
                                 ssm_c_re[l], ssm_c_im[l])
        for grp in groups:
            batch = grp["batch"]
            s0 = None if grp["s0"] is None else _init_states(grp["s0"][:, l])
            if l == 0:
                x, u, gb, zc = _inproj(grp["x"], grp["mod"], g_mix3, w_in_t, l, batch_major=True)
            else:
                x = grp["x"]
                u, gb, zc = _inproj_ahead(x, grp["mod"], g_mix3, w_in_t, l)
            y, fin = _s5(u, bc, cc, toep, a, ssm_d3, s0, batch, l)
            x1, h2 = _mixout(y, gb, zc, x, grp["mod"], w_glu_b, w_out_b, conv_w, conv_b3, g_ffn3,
                             batch, grp["seg"], l)
            grp["x"] = _ffn(h2, x1, grp["mod"], w_gate_up_b, w_down_b, g_final[None], l, batch,
                            final=(l == DEPTH - 1))
            if grp["s0"] is None:
                finals.append(_final_states(fin))

    new_state = jnp.stack(finals, axis=1).astype(x_prompt.dtype)
    return (groups[0]["x"], groups[1]["x"], new_state)
```

```python
import functools

import jax
import jax.numpy as jnp
from jax import lax
from jax.experimental import pallas as pl
from jax.experimental.pallas import tpu as pltpu

D_MODEL = 2048
DEPTH = 2
GRID_W = 64
D_SSM = 1024
D_CONV = 1024
SSM_H = 16
SSM_GROUPS = 64
SSM_STATE = 64
N_DIR = 2
D_FF = 5632
N_MOD = 6
EPS = 1e-6
LAM_RE_MAX = -1e-4

SUBLANES = 8
LANES = 128
GROUPS_PER_BLOCK = LANES // SSM_H
N_GROUP_BLOCKS = SSM_GROUPS // GROUPS_PER_BLOCK
PAIRS = GROUPS_PER_BLOCK // 2
STATE_LANES = GROUPS_PER_BLOCK * 2 * SSM_STATE
SCAN_STEPS = 64
T_BLK = 4
VMEM_LIMIT = 56 * 1024 * 1024

TM_INPROJ = 1024
TM_MIXOUT = 512
TM_FFN = 512
TN_INPROJ = 1024
TF_FFN = 512

F32 = jnp.float32
BF16 = jnp.bfloat16


def _params(*sem):
    return pltpu.CompilerParams(dimension_semantics=sem, vmem_limit_bytes=VMEM_LIMIT)


def _resident(shape, index_map):
    return pl.BlockSpec(shape, index_map, pipeline_mode=pl.Buffered(1))


def _rows8(a):
    return a.reshape(a.shape[0] // SUBLANES, SUBLANES, a.shape[1])


def _norm_mod(x, gain, scale, shift):
    ms = jnp.mean(x * x, axis=-1, keepdims=True)
    y = x * lax.rsqrt(ms + EPS) * gain
    h = _rows8(y) * (1.0 + scale)[None] + shift[None]
    return h.reshape(x.shape)


def _cast_kernel(w_ref, o_ref):
    o_ref[...] = w_ref[...].astype(BF16).reshape(o_ref.shape)


def _cast_col_tiles(w, tn, parts=1):
    _, k, n = w.shape
    nt = n // tn
    q = tn // parts
    return pl.pallas_call(
        _cast_kernel,
        grid=(DEPTH, nt, parts),
        in_specs=[pl.BlockSpec((1, k, q), lambda l, j, p: (l, 0, p * nt + j))],
        out_specs=pl.BlockSpec((1, 1, k, q), lambda l, j, p: (l, j, 0, p)),
        out_shape=jax.ShapeDtypeStruct((DEPTH, nt, k, tn), BF16),
        compiler_params=_params("parallel", "parallel", "parallel"),
        name="cast_col_tiles",
    )(w)


def _cast_pair_kernel(a_ref, b_ref, o_ref):
    tn = a_ref.shape[2]
    o_ref[0, 0, :, :tn] = a_ref[0].astype(BF16)
    o_ref[0, 0, :, tn:] = b_ref[0].astype(BF16)


def _cast_pair_col_tiles(wa, wb, tn):
    _, k, n = wa.shape
    spec = pl.BlockSpec((1, k, tn), lambda l, j: (l, 0, j))
    return pl.pallas_call(
        _cast_pair_kernel,
        grid=(DEPTH, n // tn),
        in_specs=[spec, spec],
        out_specs=pl.BlockSpec((1, 1, k, 2 * tn), lambda l, j: (l, j, 0, 0)),
        out_shape=jax.ShapeDtypeStruct((DEPTH, n // tn, k, 2 * tn), BF16),
        compiler_params=_params("parallel", "parallel"),
        name="cast_pair_col_tiles",
    )(wa, wb)


def _cast_rows(w, tk):
    _, k, n = w.shape
    return pl.pallas_call(
        _cast_kernel,
        grid=(DEPTH, k // tk),
        in_specs=[pl.BlockSpec((1, tk, n), lambda l, i: (l, i, 0))],
        out_specs=pl.BlockSpec((1, tk, n), lambda l, i: (l, i, 0)),
        out_shape=jax.ShapeDtypeStruct((DEPTH, k, n), BF16),
        compiler_params=_params("parallel", "parallel"),
        name="cast_rows",
    )(w)


def _mod_kernel(cond_ref, w_ref, b_ref, o_ref):
    s = jax.nn.silu(cond_ref[...]).astype(BF16)
    o_ref[0] = jnp.dot(s, w_ref[0].astype(BF16), preferred_element_type=F32) + b_ref[0]


def _modulation(cond, w_ada, b_ada):
    tn = 1024
    n = N_MOD * D_MODEL
    return pl.pallas_call(
        _mod_kernel,
        grid=(DEPTH, n // tn),
        in_specs=[
            pl.BlockSpec((16, D_MODEL), lambda l, j: (0, 0)),
            pl.BlockSpec((1, D_MODEL, tn), lambda l, j: (l, 0, j)),
            pl.BlockSpec((1, 1, tn), lambda l, j: (l, 0, j)),
        ],
        out_specs=pl.BlockSpec((1, 16, tn), lambda l, j: (l, 0, j)),
        out_shape=jax.ShapeDtypeStruct((DEPTH, 16, n), F32),
        compiler_params=_params("parallel", "parallel"),
        name="adaln_modulation",
    )(cond, w_ada, b_ada.reshape(DEPTH, 1, n))


def _inproj_kernel(x_ref, mod_ref, g_ref, w_ref, *out_refs, batch_major):
    if batch_major:
        xt_ref, u_ref, gb_ref, zc_ref, h_even, h_odd = out_refs
    else:
        u_ref, gb_ref, zc_ref, h_even, h_odd = out_refs
    i = pl.program_id(0)
    j = pl.program_id(1)
    tq = h_even.shape[0] // pl.num_programs(1)

    def prepare(h_next):
        if batch_major:
            x = jnp.concatenate([x_ref[:, t, :] for t in range(x_ref.shape[1])], axis=0)
            xt_ref[...] = x
        else:
            x = x_ref[...]
        h = _norm_mod(x, g_ref[0], mod_ref[0, 1], mod_ref[0, 0])
        h_next[pl.ds(pl.multiple_of(j * tq, tq), tq), :] = h.astype(BF16)

    def multiply(h_cur):
        z = jnp.dot(h_cur[...], w_ref[0, 0], preferred_element_type=F32)
        q = z.shape[1] // 4
        u_ref[...] = z[:, :q]
        gb_ref[...] = z[:, q:2 * q]
        zc_ref[...] = z[:, 2 * q:3 * q] * z[:, 3 * q:]

    n = pl.num_programs(0) - 1
    bufs = (h_even, h_odd)

    @pl.when(i == 0)
    def _():
        prepare(h_even)

    for parity in range(2):
        @pl.when((i > 0) & (i < n) & (i % 2 == parity))
        def _():
            prepare(bufs[parity])
            multiply(bufs[1 - parity])

        @pl.when((i == n) & (i % 2 == parity))
        def _():
            multiply(bufs[1 - parity])


def _inproj(x, mod, g_mix, w_in_tiled, layer, batch_major):
    tm, tn = TM_INPROJ, TN_INPROJ
    q = tn // 4
    nj = 4 * D_SSM // tn
    tq = tm // nj
    r = x.shape[0] * x.shape[1] if batch_major else x.shape[0]
    n = r // tm
    quarter = lambda i, j: jnp.minimum(i * nj + j, n * nj - 1)
    if batch_major:
        b = x.shape[0]
        x_spec = pl.BlockSpec((b, tq // b, D_MODEL), lambda i, j: (0, quarter(i, j), 0))
    else:
        x_spec = pl.BlockSpec((tq, D_MODEL), lambda i, j: (quarter(i, j), 0))
    out = jax.ShapeDtypeStruct((r, D_SSM), F32)
    out_specs = [pl.BlockSpec((tm, q), lambda i, j: (jnp.maximum(i - 1, 0), jnp.where(i == 0, 0, j)))] * 3
    out_shape = [out, out, out]
    if batch_major:
        out_specs = [pl.BlockSpec((tq, D_MODEL), lambda i, j: (quarter(i, j), 0))] + out_specs
        out_shape = [jax.ShapeDtypeStruct((r, D_MODEL), F32)] + out_shape
    return pl.pallas_call(
        functools.partial(_inproj_kernel, batch_major=batch_major),
        grid=(n + 1, nj),
        in_specs=[
            x_spec,
            pl.BlockSpec((1, N_MOD, SUBLANES, D_MODEL), lambda i, j: (layer, 0, 0, 0)),
            pl.BlockSpec((1, 1, D_MODEL), lambda i, j: (layer, 0, 0)),
            pl.BlockSpec((1, 1, D_MODEL, tn), lambda i, j: (layer, j, 0, 0)),
        ],
        out_specs=out_specs,
        out_shape=out_shape,
        scratch_shapes=[pltpu.VMEM((tm, D_MODEL), BF16), pltpu.VMEM((tm, D_MODEL), BF16)],
        compiler_params=_params("arbitrary", "arbitrary"),
        name="inproj",
    )(x, mod, g_mix, w_in_tiled)


def _s5_kernel(*refs, batch, has_init):
    if has_init:
        u_ref, bc_ref, cc_ref, toep_ref, a_ref, d_ref, s0_ref, y_ref, fin_ref, sf, sb, xs = refs
    else:
        u_ref, bc_ref, cc_ref, toep_ref, a_ref, d_ref, y_ref, fin_ref, sf, sb, xs = refs
        s0_ref = None
    rows = u_ref.shape[0]
    rc = SCAN_STEPS * batch
    n_chunks = rows // rc
    nblk = SCAN_STEPS // T_BLK
    rblk = nblk * batch
    halves = batch // SUBLANES
    tl = T_BLK * LANES

    y_ref[...] = d_ref[0] * u_ref[...]
    if has_init:
        xs[...] = s0_ref[0]
    else:
        xs[...] = jnp.zeros_like(xs)

    def gather(r0):
        uc = u_ref[pl.ds(r0, rc), :].reshape(nblk, T_BLK, batch, LANES)
        return jnp.concatenate([uc[:, i].reshape(rblk, LANES) for i in range(T_BLK)], axis=1).astype(BF16)

    def scatter_add(r0, yb):
        parts = [yb[:, i * LANES:(i + 1) * LANES].reshape(nblk, 1, batch, LANES) for i in range(T_BLK)]
        y_ref[pl.ds(r0, rc), :] += jnp.concatenate(parts, axis=1).reshape(rc, LANES)

    def chunk(ci, carry):
        rf = pl.multiple_of(ci * rc, rc)
        rb = pl.multiple_of((n_chunks - 1 - ci) * rc, rc)
        lf = gather(rf)
        lb = gather(rb)
        sf[...] = jnp.dot(lf, bc_ref[0, 0, :, :STATE_LANES], preferred_element_type=F32)
        sb[...] = jnp.dot(lb, bc_ref[0, 0, :, STATE_LANES:], preferred_element_type=F32)
        for half in range(halves):
            hs = slice(half * SUBLANES, (half + 1) * SUBLANES)
            for pair in range(PAIRS):
                re = slice(pair * LANES, (pair + 1) * LANES)
                im = slice(STATE_LANES // 2 + pair * LANES, STATE_LANES // 2 + (pair + 1) * LANES)
                for d, s in ((0, sf), (1, sb)):
                    al = slice(d * PAIRS * LANES + pair * LANES, d * PAIRS * LANES + (pair + 1) * LANES)
                    ar = a_ref[0, 0, 0, :, al]
                    ai = a_ref[0, 0, 1, :, al]
                    xr = xs[d, hs, re]
                    xi = xs[d, hs, im]
                    for t in range(nblk):
                        blk = t if d == 0 else nblk - 1 - t
                        rs = slice(blk * batch + half * SUBLANES, blk * batch + (half + 1) * SUBLANES)
                        sr = s[rs, re]
                        si = s[rs, im]
                        s[rs, re] = xr
                        s[rs, im] = xi
                        xr, xi = ar * xr - ai * xi + sr, ar * xi + ai * xr + si
                    xs[d, hs, re] = xr
                    xs[d, hs, im] = xi
        yf = jnp.dot(sf[...].astype(BF16), cc_ref[0, 0, :STATE_LANES, :], preferred_element_type=F32)
        yf += jnp.dot(lf, toep_ref[0, 0], preferred_element_type=F32)
        yb = jnp.dot(sb[...].astype(BF16), cc_ref[0, 0, STATE_LANES:, :], preferred_element_type=F32)
        scatter_add(rf, yf)
        scatter_add(rb, yb)
        return carry

    lax.fori_loop(0, n_chunks, chunk, 0)
    fin_ref[0] = xs[...]


def _s5(u, bc, cc, toep, a, d_skip, s0, batch, layer):
    r = u.shape[0]
    rblk = (SCAN_STEPS // T_BLK) * batch
    tl = T_BLK * LANES
    has_init = s0 is not None
    in_specs = [
        pl.BlockSpec((r, LANES), lambda g: (0, g)),
        pl.BlockSpec((1, 1, tl, N_DIR * STATE_LANES), lambda g: (layer, g, 0, 0)),
        pl.BlockSpec((1, 1, N_DIR * STATE_LANES, tl), lambda g: (layer, g, 0, 0)),
        pl.BlockSpec((1, 1, tl, tl), lambda g: (layer, g, 0, 0)),
        pl.BlockSpec((1, 1, 2, SUBLANES, N_DIR * PAIRS * LANES), lambda g: (layer, g, 0, 0, 0)),
        pl.BlockSpec((1, 1, LANES), lambda g: (layer, 0, g)),
    ]
    args = [u, bc, cc, toep, a, d_skip]
    if has_init:
        in_specs.append(pl.BlockSpec((1, N_DIR, batch, STATE_LANES), lambda g: (g, 0, 0, 0)))
        args.append(s0)
    return pl.pallas_call(
        functools.partial(_s5_kernel, batch=batch, has_init=has_init),
        grid=(N_GROUP_BLOCKS,),
        in_specs=in_specs,
        out_specs=[
            pl.BlockSpec((r, LANES), lambda g: (0, g)),
            pl.BlockSpec((1, N_DIR, batch, STATE_LANES), lambda g: (g, 0, 0, 0)),
        ],
        out_shape=[
            jax.ShapeDtypeStruct((r, D_SSM), F32),
            jax.ShapeDtypeStruct((N_GROUP_BLOCKS, N_DIR, batch, STATE_LANES), F32),
        ],
        scratch_shapes=[
            pltpu.VMEM((rblk, STATE_LANES), F32),
            pltpu.VMEM((rblk, STATE_LANES), F32),
            pltpu.VMEM((N_DIR, batch, STATE_LANES), F32),
        ],
        compiler_params=_params("parallel"),
        name="s5_mixer",
    )(*args)


def _mixout_kernel(y_ref, gb_ref, zc_ref, zp_ref, zn_ref, x_ref, mod_ref, wglu_ref, wout_ref,
                   cw_ref, cb_ref, g_ref, x1_ref, h2_ref, *, batch, seg):
    tm = x_ref.shape[0]
    tl = tm // batch
    l0 = pl.program_id(0) * tl
    prev_ok = (l0 % seg != 0).astype(F32)
    next_ok = ((l0 + tl) % seg != 0).astype(F32)

    y = jax.nn.gelu(y_ref[...])
    y_ssm = y * jax.nn.sigmoid(jnp.dot(y.astype(BF16), wglu_ref[0], preferred_element_type=F32))

    zc = zc_ref[...]
    prev = jnp.concatenate([zp_ref[...] * prev_ok, zc[:tm - batch]], axis=0)
    nxt = jnp.concatenate([zc[batch:], zn_ref[...] * next_ok], axis=0)
    conv = prev * cw_ref[0, 0:1, :] + zc * cw_ref[0, 1:2, :] + nxt * cw_ref[0, 2:3, :] + cb_ref[0]
    y_conv = gb_ref[...] * conv

    out = jnp.dot(y_ssm.astype(BF16), wout_ref[0, :D_SSM, :], preferred_element_type=F32)
    out += jnp.dot(y_conv.astype(BF16), wout_ref[0, D_SSM:, :], preferred_element_type=F32)
    x1 = (_rows8(x_ref[...]) + mod_ref[0, 2][None] * _rows8(out)).reshape(tm, D_MODEL)
    x1_ref[...] = x1
    h2_ref[...] = _norm_mod(x1, g_ref[0], mod_ref[0, 4], mod_ref[0, 3]).astype(BF16)


def _mixout(y, gb, zc, x, mod, w_glu, w_out, conv_w, conv_b, g_ffn, batch, seg, layer):
    r = x.shape[0]
    tm = TM_MIXOUT
    assert seg % (tm // batch) == 0
    hb = tm // batch
    n_hb = r // batch
    row = lambda i: (i, 0)
    lay = lambda i: (layer, 0, 0)
    return pl.pallas_call(
        functools.partial(_mixout_kernel, batch=batch, seg=seg),
        grid=(r // tm,),
        in_specs=[
            pl.BlockSpec((tm, D_SSM), row),
            pl.BlockSpec((tm, D_CONV), row),
            pl.BlockSpec((tm, D_CONV), row),
            pl.BlockSpec((batch, D_CONV), lambda i: (jnp.maximum(i * hb - 1, 0), 0)),
            pl.BlockSpec((batch, D_CONV), lambda i: (jnp.minimum((i + 1) * hb, n_hb - 1), 0)),
            pl.BlockSpec((tm, D_MODEL), row),
            _resident((1, N_MOD, SUBLANES, D_MODEL), lambda i: (layer, 0, 0, 0)),
            _resident((1, D_SSM, D_SSM), lay),
            _resident((1, D_MODEL, D_MODEL), lay),
            _resident((1, 3, D_CONV), lay),
            _resident((1, 1, D_CONV), lay),
            _resident((1, 1, D_MODEL), lay),
        ],
        out_specs=[pl.BlockSpec((tm, D_MODEL), row), pl.BlockSpec((tm, D_MODEL), row)],
        out_shape=[jax.ShapeDtypeStruct((r, D_MODEL), F32), jax.ShapeDtypeStruct((r, D_MODEL), BF16)],
        compiler_params=_params("parallel"),
        name="mixer_out",
    )(y, gb, zc, zc, zc, x, mod, w_glu, w_out, conv_w, conv_b, g_ffn)


def _ffn_kernel(h_ref, x_ref, mod_ref, wgu_ref, wd_ref, gf_ref, o_ref, acc, *, final):
    k = pl.program_id(1)

    @pl.when(k == 0)
    def _():
        acc[...] = jnp.zeros_like(acc)

    gate_up = jnp.dot(h_ref[...], wgu_ref[0, 0], preferred_element_type=F32)
    tf = gate_up.shape[1] // 2
    act = (jax.nn.silu(gate_up[:, :tf]) * gate_up[:, tf:]).astype(BF16)
    acc[...] += jnp.dot(act, wd_ref[0], preferred_element_type=F32)

    @pl.when(k == pl.num_programs(1) - 1)
    def _():
        x2 = (_rows8(x_ref[...]) + mod_ref[0, 5][None] * _rows8(acc[...])).reshape(x_ref.shape)
        if final:
            ms = jnp.mean(x2 * x2, axis=-1, keepdims=True)
            x2 = x2 * lax.rsqrt(ms + EPS) * gf_ref[...]
            b = o_ref.shape[0]
            for t in range(o_ref.shape[1]):
                o_ref[:, t, :] = x2[t * b:(t + 1) * b, :]
        else:
            o_ref[...] = x2


def _ffn(h2, x1, mod, w_gate_up, w_down, g_final, layer, batch, final):
    r = x1.shape[0]
    tm, tf = TM_FFN, TF_FFN
    if final:
        out_spec = pl.BlockSpec((batch, tm // batch, D_MODEL), lambda i, k: (0, i, 0))
        out_shape = jax.ShapeDtypeStruct((batch, r // batch, D_MODEL), F32)
    else:
        out_spec = pl.BlockSpec((tm, D_MODEL), lambda i, k: (i, 0))
        out_shape = jax.ShapeDtypeStruct((r, D_MODEL), F32)
    return pl.pallas_call(
        functools.partial(_ffn_kernel, final=final),
        grid=(r // tm, D_FF // tf),
        in_specs=[
            pl.BlockSpec((tm, D_MODEL), lambda i, k: (i, 0)),
            pl.BlockSpec((tm, D_MODEL), lambda i, k: (i, 0)),
            pl.BlockSpec((1, N_MOD, SUBLANES, D_MODEL), lambda i, k: (layer, 0, 0, 0)),
            pl.BlockSpec((1, 1, D_MODEL, 2 * tf), lambda i, k: (layer, k, 0, 0)),
            pl.BlockSpec((1, tf, D_MODEL), lambda i, k: (layer, k, 0)),
            pl.BlockSpec((1, D_MODEL), lambda i, k: (0, 0)),
        ],
        out_specs=out_spec,
        out_shape=out_shape,
        scratch_shapes=[pltpu.VMEM((tm, D_MODEL), F32)],
        compiler_params=_params("parallel", "arbitrary"),
        name="ffn",
    )(h2, x1, mod, w_gate_up, w_down, g_final)


def _s5_build_kernel(pw_ref, bb_ref, cs_ref, kt_ref, bc_ref, cc_ref, toep_ref):
    row_group = lax.broadcasted_iota(jnp.int32, (LANES, LANES), 0) // SSM_H
    lane = lax.broadcasted_iota(jnp.int32, (LANES, LANES), 1)
    row_group_s = lax.broadcasted_iota(jnp.int32, (LANES, SSM_STATE), 0) // SSM_H

    def per_row(p):
        out = jnp.zeros((LANES, SSM_STATE), F32)
        for g in range(GROUPS_PER_BLOCK):
            out = jnp.where(row_group_s == g, p[g:g + 1, :], out)
        return out

    dup_state = (lax.broadcasted_iota(jnp.int32, (SSM_STATE, LANES), 1) % SSM_STATE
                 == lax.broadcasted_iota(jnp.int32, (SSM_STATE, LANES), 0)).astype(BF16)
    dup_h = (lax.broadcasted_iota(jnp.int32, (SSM_H, LANES), 1) % SSM_H
             == lax.broadcasted_iota(jnp.int32, (SSM_H, LANES), 0)).astype(BF16)
    keep = [(row_group // 2 == q) & (lane // SSM_STATE == row_group % 2) for q in range(PAIRS)]
    for d in range(N_DIR):
        b_re, b_im = bb_ref[0, d, 0, 0], bb_ref[0, d, 1, 0]
        c_re, c_im = cs_ref[0, d, 0, 0], cs_ref[0, d, 1, 0]
        for i in range(T_BLK):
            pr, pi = per_row(pw_ref[0, d, 0, i, 0, 0]), per_row(pw_ref[0, d, 0, i, 1, 0])
            qr, qi = per_row(pw_ref[0, d, 1, i, 0, 0]), per_row(pw_ref[0, d, 1, i, 1, 0])
            src_b = (b_re * pr - b_im * pi, b_re * pi + b_im * pr)
            src_c = (c_re * qr - c_im * qi, -(c_re * qi + c_im * qr))
            for r in range(2):
                wide_b = jnp.dot(src_b[r].astype(BF16), dup_state, preferred_element_type=F32)
                wide_c = jnp.dot(src_c[r].astype(BF16), dup_state, preferred_element_type=F32)
                for q in range(PAIRS):
                    c0 = d * STATE_LANES + r * (STATE_LANES // 2) + q * LANES
                    bc_ref[0, 0, i * LANES:(i + 1) * LANES, c0:c0 + LANES] = (
                        jnp.where(keep[q], wide_b, 0.0).astype(BF16))
                    cc_ref[0, 0, c0:c0 + LANES, i * LANES:(i + 1) * LANES] = (
                        jnp.where(keep[q], wide_c, 0.0).T.astype(BF16))
    diag = row_group == lane // SSM_H
    for i_in in range(T_BLK):
        for i_out in range(T_BLK):
            wide = jnp.dot(kt_ref[0, 0, i_in, i_out].astype(BF16), dup_h, preferred_element_type=F32)
            toep_ref[0, 0, i_in * LANES:(i_in + 1) * LANES, i_out * LANES:(i_out + 1) * LANES] = (
                jnp.where(diag, wide, 0.0).astype(BF16))


def _cpow(lr, li, dt, k):
    m = jnp.exp(lr * dt * k)
    return m * jnp.cos(li * dt * k), m * jnp.sin(li * dt * k)


def _s5_sources(lam_re, lam_im, log_dt, b_re, b_im, c_re, c_im):
    hi = lax.Precision.HIGHEST
    t = T_BLK
    nb = N_GROUP_BLOCKS
    lr = jnp.minimum(lam_re, LAM_RE_MAX)
    li = lam_im
    dt = jnp.exp(log_dt)[..., None]
    ar, ai = _cpow(lr, li, dt, 1.0)
    den = lr * lr + li * li
    qr = (((ar - 1.0) * lr + ai * li) / den)[:, :, None, :]
    qi = ((ai * lr - (ar - 1.0) * li) / den)[:, :, None, :]
    b_re_t = b_re.transpose(0, 1, 3, 2)
    b_im_t = b_im.transpose(0, 1, 3, 2)
    bbr = qr * b_re_t - qi * b_im_t
    bbi = qr * b_im_t + qi * b_re_t

    steps = jnp.arange(t, dtype=F32)
    e_in = jnp.stack([t - 1.0 - steps, steps], axis=0)[:, :, None, None]
    e_out = jnp.stack([steps + 1.0, t - steps], axis=0)[:, :, None, None]
    e_lag = jnp.stack([steps, steps], axis=0)[:, :, None, None]
    lr4, li4, dt4 = lr[:, None], li[:, None], dt[:, None]
    pw = jnp.stack([jnp.stack(_cpow(lr4, li4, dt4, e_in), axis=2),
                    jnp.stack(_cpow(lr4, li4, dt4, e_out), axis=2)], axis=1)
    pw = pw.reshape(N_DIR, 2, t, 2, nb, GROUPS_PER_BLOCK, SSM_STATE)
    bbt = jnp.stack([bbr, bbi], axis=1).reshape(N_DIR, 2, nb, LANES, SSM_STATE)
    cs = jnp.stack([c_re, c_im], axis=1).reshape(N_DIR, 2, nb, LANES, SSM_STATE)

    pr, pi = _cpow(lr4, li4, dt4, e_lag)
    pr, pi = pr[:, :, :, None, :], pi[:, :, :, None, :]
    lgr = c_re[:, None] * pr - c_im[:, None] * pi
    lgi = c_re[:, None] * pi + c_im[:, None] * pr
    klag = (jnp.einsum("dlgop,dghp->dlgho", lgr, bbr, precision=hi)
            - jnp.einsum("dlgop,dghp->dlgho", lgi, bbi, precision=hi))
    idx = jnp.arange(t)
    lag = idx[None, :] - idx[:, None]
    m5 = lambda m: m[:, :, None, None, None]
    kt = (jnp.where(m5(lag >= 0), klag[0][jnp.clip(lag, 0, t - 1)], 0.0)
          + jnp.where(m5(lag <= 0), klag[1][jnp.clip(-lag, 0, t - 1)], 0.0))
    kt = kt.reshape(t, t, nb, LANES, SSM_H).transpose(2, 0, 1, 3, 4)

    atr, ati = _cpow(lr, li, dt, float(t))
    a = jnp.stack([atr, ati], axis=0)
    a = a.reshape(2, N_DIR, nb, PAIRS * LANES).transpose(2, 0, 1, 3).reshape(nb, 2, 1, N_DIR * PAIRS * LANES)
    a = jnp.broadcast_to(a, (nb, 2, SUBLANES, N_DIR * PAIRS * LANES))
    return pw, bbt, cs, kt, a


def _s5_matrices(lam_re, lam_im, log_dt, b_re, b_im, c_re, c_im):
    t = T_BLK
    nb = N_GROUP_BLOCKS
    pw, bbt, cs, kt, a = jax.vmap(_s5_sources)(lam_re, lam_im, log_dt, b_re, b_im, c_re, c_im)
    coef_spec = pl.BlockSpec((1, N_DIR, 2, 1, LANES, SSM_STATE), lambda l, g: (l, 0, 0, g, 0, 0))
    shape = lambda r, c: jax.ShapeDtypeStruct((DEPTH, nb, r, c), BF16)
    spec = lambda r, c: pl.BlockSpec((1, 1, r, c), lambda l, g: (l, g, 0, 0))
    bc, cc, toep = pl.pallas_call(
        _s5_build_kernel,
        grid=(DEPTH, nb),
        in_specs=[
            pl.BlockSpec((1, N_DIR, 2, t, 2, 1, GROUPS_PER_BLOCK, SSM_STATE), lambda l, g: (l, 0, 0, 0, 0, g, 0, 0)),
            coef_spec,
            coef_spec,
            pl.BlockSpec((1, 1, t, t, LANES, SSM_H), lambda l, g: (l, g, 0, 0, 0, 0)),
        ],
        out_specs=[spec(t * LANES, N_DIR * STATE_LANES), spec(N_DIR * STATE_LANES, t * LANES),
                   spec(t * LANES, t * LANES)],
        out_shape=[shape(t * LANES, N_DIR * STATE_LANES), shape(N_DIR * STATE_LANES, t * LANES),
                   shape(t * LANES, t * LANES)],
        compiler_params=_params("parallel", "parallel"),
        name="s5_build",
    )(pw, bbt, cs, kt)
    return bc, cc, toep, a


def _init_states(st):
    b = st.shape[0]
    s = st.reshape(b, N_DIR, 2, N_GROUP_BLOCKS, STATE_LANES // 2)
    return s.transpose(3, 1, 0, 2, 4).reshape(N_GROUP_BLOCKS, N_DIR, b, STATE_LANES)


def _final_states(fin):
    b = fin.shape[2]
    s = fin.reshape(N_GROUP_BLOCKS, N_DIR, b, 2, STATE_LANES // 2)
    return s.transpose(2, 1, 3, 0, 4).reshape(b, N_DIR, 2, SSM_GROUPS, SSM_STATE)


def kernel(x_prompt, x_sample, state_ssm, c, c_ctx, w_ada, b_ada, g_mix, w_in, ssm_lam_re, ssm_lam_im, ssm_log_dt, ssm_b_re, ssm_b_im, ssm_c_re, ssm_c_im, ssm_d, w_glu, conv_w, conv_b, w_out, g_ffn, w_gate, w_up, w_down, g_final):
    n_ctx, l_ctx, _ = x_prompt.shape
    n_dec, l_dec, _ = x_sample.shape

    cond = jnp.zeros((16, D_MODEL), F32).at[0].set(c_ctx).at[1:1 + n_dec].set(c)
    mod = _modulation(cond, w_ada, b_ada).reshape(DEPTH, 16, N_MOD, D_MODEL)
    mod_ctx = jnp.broadcast_to(mod[:, 0, :, None, :], (DEPTH, N_MOD, SUBLANES, D_MODEL))
    mod_dec = mod[:, 1:1 + n_dec].transpose(0, 2, 1, 3)

    w_in_t = _cast_col_tiles(w_in, TN_INPROJ, parts=4)
    w_glu_b = _cast_rows(w_glu, 512)
    w_out_b = _cast_rows(w_out, 512)
    w_gate_up_b = _cast_pair_col_tiles(w_gate, w_up, TF_FFN)
    w_down_b = _cast_rows(w_down, TF_FFN)
    g_mix3 = g_mix[:, None, :]
    g_ffn3 = g_ffn[:, None, :]
    conv_b3 = conv_b[:, None, :]
    ssm_d3 = ssm_d[:, None, :]

    groups = [
        dict(x=x_prompt, batch=n_ctx, seg=l_ctx, mod=mod_ctx, s0=None),
        dict(x=x_sample, batch=n_dec, seg=GRID_W, mod=mod_dec, s0=state_ssm.astype(F32)),
    ]

    bc, cc, toep, a = _s5_matrices(ssm_lam_re, ssm_lam_im, ssm_log_dt, ssm_b_re, ssm_b_im, ssm_c_re, ssm_c_im)
    finals = []
    for l in range(DEPTH):
        for grp in groups:
            batch = grp["batch"]
            s0 = None if grp["s0"] is None else _init_states(grp["s0"][:, l])
            if l == 0:
                x, u, gb, zc = _inproj(grp["x"], grp["mod"], g_mix3, w_in_t, l, batch_major=True)
            else:
                x = grp["x"]
                u, gb, zc = _inproj(x, grp["mod"], g_mix3, w_in_t, l, batch_major=False)
            y, fin = _s5(u, bc, cc, toep, a, ssm_d3, s0, batch, l)
            x1, h2 = _mixout(y, gb, zc, x, grp["mod"], w_glu_b, w_out_b, conv_w, conv_b3, g_ffn3,
                             batch, grp["seg"], l)
            grp["x"] = _ffn(h2, x1, grp["mod"], w_gate_up_b, w_down_b, g_final[None], l, batch,
                            final=(l == DEPTH - 1))
            if grp["s0"] is None:
                finals.append(_final_states(fin))

    new_state = jnp.stack(finals, axis=1).astype(x_prompt.dtype)
    return (groups[0]["x"], groups[1]["x"], new_state)
```

```python
import functools

import jax
import jax.numpy as jnp
from jax import lax
from jax.experimental import pallas as pl
from jax.experimental.pallas import tpu as pltpu

D_MODEL = 2048
DEPTH = 2
GRID_W = 64
D_SSM = 1024
D_CONV = 1024
SSM_H = 16
SSM_GROUPS = 64
SSM_STATE = 64
N_DIR = 2
D_FF = 5632
N_MOD = 6
EPS = 1e-6
LAM_RE_MAX = -1e-4

SUBLANES = 8
LANES = 128
GROUPS_PER_BLOCK = LANES // SSM_H
N_GROUP_BLOCKS = SSM_GROUPS // GROUPS_PER_BLOCK
PAIRS = GROUPS_PER_BLOCK // 2
STATE_LANES = GROUPS_PER_BLOCK * 2 * SSM_STATE
SCAN_STEPS = 64
T_BLK = 4
VMEM_LIMIT = 56 * 1024 * 1024

TM_INPROJ = 1024
TM_MIXOUT = 512
TM_FFN = 512
TN_INPROJ = 1024
TF_FFN = 512

F32 = jnp.float32
BF16 = jnp.bfloat16


def _params(*sem):
    return pltpu.CompilerParams(dimension_semantics=sem, vmem_limit_bytes=VMEM_LIMIT)


def _resident(shape, index_map):
    return pl.BlockSpec(shape, index_map, pipeline_mode=pl.Buffered(1))


def _rows8(a):
    return a.reshape(a.shape[0] // SUBLANES, SUBLANES, a.shape[1])


def _norm_mod(x, gain, scale, shift):
    ms = jnp.mean(x * x, axis=-1, keepdims=True)
    y = x * lax.rsqrt(ms + EPS) * gain
    h = _rows8(y) * (1.0 + scale)[None] + shift[None]
    return h.reshape(x.shape)


def _cast_kernel(w_ref, o_ref):
    o_ref[...] = w_ref[...].astype(BF16).reshape(o_ref.shape)


def _cast_col_tiles(w, tn, parts=1):
    _, k, n = w.shape
    nt = n // tn
    q = tn // parts
    return pl.pallas_call(
        _cast_kernel,
        grid=(DEPTH, nt, parts),
        in_specs=[pl.BlockSpec((1, k, q), lambda l, j, p: (l, 0, p * nt + j))],
        out_specs=pl.BlockSpec((1, 1, k, q), lambda l, j, p: (l, j, 0, p)),
        out_shape=jax.ShapeDtypeStruct((DEPTH, nt, k, tn), BF16),
        compiler_params=_params("parallel", "parallel", "parallel"),
        name="cast_col_tiles",
    )(w)


def _cast_pair_kernel(a_ref, b_ref, o_ref):
    tn = a_ref.shape[2]
    o_ref[0, 0, :, :tn] = a_ref[0].astype(BF16)
    o_ref[0, 0, :, tn:] = b_ref[0].astype(BF16)


def _cast_pair_col_tiles(wa, wb, tn):
    _, k, n = wa.shape
    spec = pl.BlockSpec((1, k, tn), lambda l, j: (l, 0, j))
    return pl.pallas_call(
        _cast_pair_kernel,
        grid=(DEPTH, n // tn),
        in_specs=[spec, spec],
        out_specs=pl.BlockSpec((1, 1, k, 2 * tn), lambda l, j: (l, j, 0, 0)),
        out_shape=jax.ShapeDtypeStruct((DEPTH, n // tn, k, 2 * tn), BF16),
        compiler_params=_params("parallel", "parallel"),
        name="cast_pair_col_tiles",
    )(wa, wb)


def _cast_rows(w, tk):
    _, k, n = w.shape
    return pl.pallas_call(
        _cast_kernel,
        grid=(DEPTH, k // tk),
        in_specs=[pl.BlockSpec((1, tk, n), lambda l, i: (l, i, 0))],
        out_specs=pl.BlockSpec((1, tk, n), lambda l, i: (l, i, 0)),
        out_shape=jax.ShapeDtypeStruct((DEPTH, k, n), BF16),
        compiler_params=_params("parallel", "parallel"),
        name="cast_rows",
    )(w)


def _mod_kernel(cond_ref, w_ref, b_ref, o_ref):
    s = jax.nn.silu(cond_ref[...]).astype(BF16)
    o_ref[0] = jnp.dot(s, w_ref[0].astype(BF16), preferred_element_type=F32) + b_ref[0]


def _modulation(cond, w_ada, b_ada):
    tn = 1024
    n = N_MOD * D_MODEL
    return pl.pallas_call(
        _mod_kernel,
        grid=(DEPTH, n // tn),
        in_specs=[
            pl.BlockSpec((16, D_MODEL), lambda l, j: (0, 0)),
            pl.BlockSpec((1, D_MODEL, tn), lambda l, j: (l, 0, j)),
            pl.BlockSpec((1, 1, tn), lambda l, j: (l, 0, j)),
        ],
        out_specs=pl.BlockSpec((1, 16, tn), lambda l, j: (l, 0, j)),
        out_shape=jax.ShapeDtypeStruct((DEPTH, 16, n), F32),
        compiler_params=_params("parallel", "parallel"),
        name="adaln_modulation",
    )(cond, w_ada, b_ada.reshape(DEPTH, 1, n))


def _inproj_kernel(x_ref, mod_ref, g_ref, w_ref, *out_refs, batch_major):
    if batch_major:
        xt_ref, u_ref, gb_ref, zc_ref, h_even, h_odd = out_refs
    else:
        u_ref, gb_ref, zc_ref, h_even, h_odd = out_refs
    i = pl.program_id(0)
    j = pl.program_id(1)
    tq = h_even.shape[0] // pl.num_programs(1)

    def prepare(h_next):
        if batch_major:
            x = jnp.concatenate([x_ref[:, t, :] for t in range(x_ref.shape[1])], axis=0)
            xt_ref[...] = x
        else:
            x = x_ref[...]
        h = _norm_mod(x, g_ref[0], mod_ref[0, 1], mod_ref[0, 0])
        h_next[pl.ds(pl.multiple_of(j * tq, tq), tq), :] = h.astype(BF16)

    def multiply(h_cur):
        z = jnp.dot(h_cur[...], w_ref[0, 0], preferred_element_type=F32)
        q = z.shape[1] // 4
        u_ref[...] = z[:, :q]
        gb_ref[...] = z[:, q:2 * q]
        zc_ref[...] = z[:, 2 * q:3 * q] * z[:, 3 * q:]

    n = pl.num_programs(0) - 1
    bufs = (h_even, h_odd)

    @pl.when(i == 0)
    def _():
        prepare(h_even)

    for parity in range(2):
        @pl.when((i > 0) & (i < n) & (i % 2 == parity))
        def _():
            prepare(bufs[parity])
            multiply(bufs[1 - parity])

        @pl.when((i == n) & (i % 2 == parity))
        def _():
            multiply(bufs[1 - parity])


def _inproj(x, mod, g_mix, w_in_tiled, layer, batch_major):
    tm, tn = TM_INPROJ, TN_INPROJ
    q = tn // 4
    nj = 4 * D_SSM // tn
    tq = tm // nj
    r = x.shape[0] * x.shape[1] if batch_major else x.shape[0]
    n = r // tm
    quarter = lambda i, j: jnp.minimum(i * nj + j, n * nj - 1)
    if batch_major:
        b = x.shape[0]
        x_spec = pl.BlockSpec((b, tq // b, D_MODEL), lambda i, j: (0, quarter(i, j), 0))
    else:
        x_spec = pl.BlockSpec((tq, D_MODEL), lambda i, j: (quarter(i, j), 0))
    out = jax.ShapeDtypeStruct((r, D_SSM), F32)
    out_specs = [pl.BlockSpec((tm, q), lambda i, j: (jnp.maximum(i - 1, 0), jnp.where(i == 0, 0, j)))] * 3
    out_shape = [out, out, out]
    if batch_major:
        out_specs = [pl.BlockSpec((tq, D_MODEL), lambda i, j: (quarter(i, j), 0))] + out_specs
        out_shape = [jax.ShapeDtypeStruct((r, D_MODEL), F32)] + out_shape
    return pl.pallas_call(
        functools.partial(_inproj_kernel, batch_major=batch_major),
        grid=(n + 1, nj),
        in_specs=[
            x_spec,
            pl.BlockSpec((1, N_MOD, SUBLANES, D_MODEL), lambda i, j: (layer, 0, 0, 0)),
            pl.BlockSpec((1, 1, D_MODEL), lambda i, j: (layer, 0, 0)),
            pl.BlockSpec((1, 1, D_MODEL, tn), lambda i, j: (layer, j, 0, 0)),
        ],
        out_specs=out_specs,
        out_shape=out_shape,
        scratch_shapes=[pltpu.VMEM((tm, D_MODEL), BF16), pltpu.VMEM((tm, D_MODEL), BF16)],
        compiler_params=_params("arbitrary", "arbitrary"),
        name="inproj",
    )(x, mod, g_mix, w_in_tiled)


def _s5_kernel(*refs, batch, has_init):
    if has_init:
        u_ref, bc_ref, cc_ref, toep_ref, a_ref, d_ref, s0_ref, y_ref, fin_ref, sf, sb, xs = refs
    else:
        u_ref, bc_ref, cc_ref, toep_ref, a_ref, d_ref, y_ref, fin_ref, sf, sb, xs = refs
        s0_ref = None
    rows = u_ref.shape[0]
    rc = SCAN_STEPS * batch
    n_chunks = rows // rc
    nblk = SCAN_STEPS // T_BLK
    rblk = nblk * batch
    halves = batch // SUBLANES
    tl = T_BLK * LANES

    y_ref[...] = d_ref[0] * u_ref[...]
    if has_init:
        xs[...] = s0_ref[0]
    else:
        xs[...] = jnp.zeros_like(xs)

    def gather(r0):
        uc = u_ref[pl.ds(r0, rc), :].reshape(nblk, T_BLK, batch, LANES)
        return jnp.concatenate([uc[:, i].reshape(rblk, LANES) for i in range(T_BLK)], axis=1).astype(BF16)

    def scatter_add(r0, yb):
        parts = [yb[:, i * LANES:(i + 1) * LANES].reshape(nblk, 1, batch, LANES) for i in range(T_BLK)]
        y_ref[pl.ds(r0, rc), :] += jnp.concatenate(parts, axis=1).reshape(rc, LANES)

    def chunk(ci, carry):
        rf = pl.multiple_of(ci * rc, rc)
        rb = pl.multiple_of((n_chunks - 1 - ci) * rc, rc)
        lf = gather(rf)
        lb = gather(rb)
        sf[...] = jnp.dot(lf, bc_ref[0, 0, :, :STATE_LANES], preferred_element_type=F32)
        sb[...] = jnp.dot(lb, bc_ref[0, 0, :, STATE_LANES:], preferred_element_type=F32)
        for half in range(halves):
            hs = slice(half * SUBLANES, (half + 1) * SUBLANES)
            for pair in range(PAIRS):
                re = slice(pair * LANES, (pair + 1) * LANES)
                im = slice(STATE_LANES // 2 + pair * LANES, STATE_LANES // 2 + (pair + 1) * LANES)
                for d, s in ((0, sf), (1, sb)):
                    al = slice(d * PAIRS * LANES + pair * LANES, d * PAIRS * LANES + (pair + 1) * LANES)
                    ar = a_ref[0, 0, 0, :, al]
                    ai = a_ref[0, 0, 1, :, al]
                    xr = xs[d, hs, re]
                    xi = xs[d, hs, im]
                    for t in range(nblk):
                        blk = t if d == 0 else nblk - 1 - t
                        rs = slice(blk * batch + half * SUBLANES, blk * batch + (half + 1) * SUBLANES)
                        sr = s[rs, re]
                        si = s[rs, im]
                        s[rs, re] = xr
                        s[rs, im] = xi
                        xr, xi = ar * xr - ai * xi + sr, ar * xi + ai * xr + si
                    xs[d, hs, re] = xr
                    xs[d, hs, im] = xi
        yf = jnp.dot(sf[...].astype(BF16), cc_ref[0, 0, :STATE_LANES, :], preferred_element_type=F32)
        yf += jnp.dot(lf, toep_ref[0, 0], preferred_element_type=F32)
        yb = jnp.dot(sb[...].astype(BF16), cc_ref[0, 0, STATE_LANES:, :], preferred_element_type=F32)
        scatter_add(rf, yf)
        scatter_add(rb, yb)
        return carry

    lax.fori_loop(0, n_chunks, chunk, 0)
    fin_ref[0] = xs[...]


def _s5(u, bc, cc, toep, a, d_skip, s0, batch, layer):
    r = u.shape[0]
    rblk = (SCAN_STEPS // T_BLK) * batch
    tl = T_BLK * LANES
    has_init = s0 is not None
    in_specs = [
        pl.BlockSpec((r, LANES), lambda g: (0, g)),
        pl.BlockSpec((1, 1, tl, N_DIR * STATE_LANES), lambda g: (layer, g, 0, 0)),
        pl.BlockSpec((1, 1, N_DIR * STATE_LANES, tl), lambda g: (layer, g, 0, 0)),
        pl.BlockSpec((1, 1, tl, tl), lambda g: (layer, g, 0, 0)),
        pl.BlockSpec((1, 1, 2, SUBLANES, N_DIR * PAIRS * LANES), lambda g: (layer, g, 0, 0, 0)),
        pl.BlockSpec((1, 1, LANES), lambda g: (layer, 0, g)),
    ]
    args = [u, bc, cc, toep, a, d_skip]
    if has_init:
        in_specs.append(pl.BlockSpec((1, N_DIR, batch, STATE_LANES), lambda g: (g, 0, 0, 0)))
        args.append(s0)
    return pl.pallas_call(
        functools.partial(_s5_kernel, batch=batch, has_init=has_init),
        grid=(N_GROUP_BLOCKS,),
        in_specs=in_specs,
        out_specs=[
            pl.BlockSpec((r, LANES), lambda g: (0, g)),
            pl.BlockSpec((1, N_DIR, batch, STATE_LANES), lambda g: (g, 0, 0, 0)),
        ],
        out_shape=[
            jax.ShapeDtypeStruct((r, D_SSM), F32),
            jax.ShapeDtypeStruct((N_GROUP_BLOCKS, N_DIR, batch, STATE_LANES), F32),
        ],
        scratch_shapes=[
            pltpu.VMEM((rblk, STATE_LANES), F32),
            pltpu.VMEM((rblk, STATE_LANES), F32),
            pltpu.VMEM((N_DIR, batch, STATE_LANES), F32),
        ],
        compiler_params=_params("parallel"),
        name="s5_mixer",
    )(*args)


def _mixout_kernel(y_ref, gb_ref, zc_ref, zp_ref, zn_ref, x_ref, mod_ref, wglu_ref, wout_ref,
                   cw_ref, cb_ref, g_ref, x1_ref, h2_ref, *, batch, seg):
    tm = x_ref.shape[0]
    tl = tm // batch
    l0 = pl.program_id(0) * tl
    prev_ok = (l0 % seg != 0).astype(F32)
    next_ok = ((l0 + tl) % seg != 0).astype(F32)

    y = jax.nn.gelu(y_ref[...])
    y_ssm = y * jax.nn.sigmoid(jnp.dot(y.astype(BF16), wglu_ref[0], preferred_element_type=F32))

    zc = zc_ref[...]
    prev = jnp.concatenate([zp_ref[...] * prev_ok, zc[:tm - batch]], axis=0)
    nxt = jnp.concatenate([zc[batch:], zn_ref[...] * next_ok], axis=0)
    conv = prev * cw_ref[0, 0:1, :] + zc * cw_ref[0, 1:2, :] + nxt * cw_ref[0, 2:3, :] + cb_ref[0]
    y_conv = gb_ref[...] * conv

    out = jnp.dot(y_ssm.astype(BF16), wout_ref[0, :D_SSM, :], preferred_element_type=F32)
    out += jnp.dot(y_conv.astype(BF16), wout_ref[0, D_SSM:, :], preferred_element_type=F32)
    x1 = (_rows8(x_ref[...]) + mod_ref[0, 2][None] * _rows8(out)).reshape(tm, D_MODEL)
    x1_ref[...] = x1
    h2_ref[...] = _norm_mod(x1, g_ref[0], mod_ref[0, 4], mod_ref[0, 3]).astype(BF16)


def _mixout(y, gb, zc, x, mod, w_glu, w_out, conv_w, conv_b, g_ffn, batch, seg, layer):
    r = x.shape[0]
    tm = TM_MIXOUT
    assert seg % (tm // batch) == 0
    hb = tm // batch
    n_hb = r // batch
    row = lambda i: (i, 0)
    lay = lambda i: (layer, 0, 0)
    return pl.pallas_call(
        functools.partial(_mixout_kernel, batch=batch, seg=seg),
        grid=(r // tm,),
        in_specs=[
            pl.BlockSpec((tm, D_SSM), row),
            pl.BlockSpec((tm, D_CONV), row),
            pl.BlockSpec((tm, D_CONV), row),
            pl.BlockSpec((batch, D_CONV), lambda i: (jnp.maximum(i * hb - 1, 0), 0)),
            pl.BlockSpec((batch, D_CONV), lambda i: (jnp.minimum((i + 1) * hb, n_hb - 1), 0)),
            pl.BlockSpec((tm, D_MODEL), row),
            _resident((1, N_MOD, SUBLANES, D_MODEL), lambda i: (layer, 0, 0, 0)),
            _resident((1, D_SSM, D_SSM), lay),
            _resident((1, D_MODEL, D_MODEL), lay),
            _resident((1, 3, D_CONV), lay),
            _resident((1, 1, D_CONV), lay),
            _resident((1, 1, D_MODEL), lay),
        ],
        out_specs=[pl.BlockSpec((tm, D_MODEL), row), pl.BlockSpec((tm, D_MODEL), row)],
        out_shape=[jax.ShapeDtypeStruct((r, D_MODEL), F32), jax.ShapeDtypeStruct((r, D_MODEL), BF16)],
        compiler_params=_params("parallel"),
        name="mixer_out",
    )(y, gb, zc, zc, zc, x, mod, w_glu, w_out, conv_w, conv_b, g_ffn)


def _ffn_kernel(h_ref, x_ref, mod_ref, wgu_ref, wd_ref, gf_ref, o_ref, acc, *, final):
    k = pl.program_id(1)

    @pl.when(k == 0)
    def _():
        acc[...] = jnp.zeros_like(acc)

    gate_up = jnp.dot(h_ref[...], wgu_ref[0, 0], preferred_element_type=F32)
    tf = gate_up.shape[1] // 2
    act = (jax.nn.silu(gate_up[:, :tf]) * gate_up[:, tf:]).astype(BF16)
    acc[...] += jnp.dot(act, wd_ref[0], preferred_element_type=F32)

    @pl.when(k == pl.num_programs(1) - 1)
    def _():
        x2 = (_rows8(x_ref[...]) + mod_ref[0, 5][None] * _rows8(acc[...])).reshape(x_ref.shape)
        if final:
            ms = jnp.mean(x2 * x2, axis=-1, keepdims=True)
            x2 = x2 * lax.rsqrt(ms + EPS) * gf_ref[...]
            b = o_ref.shape[0]
            for t in range(o_ref.shape[1]):
                o_ref[:, t, :] = x2[t * b:(t + 1) * b, :]
        else:
            o_ref[...] = x2


def _ffn(h2, x1, mod, w_gate_up, w_down, g_final, layer, batch, final):
    r = x1.shape[0]
    tm, tf = TM_FFN, TF_FFN
    if final:
        out_spec = pl.BlockSpec((batch, tm // batch, D_MODEL), lambda i, k: (0, i, 0))
        out_shape = jax.ShapeDtypeStruct((batch, r // batch, D_MODEL), F32)
    else:
        out_spec = pl.BlockSpec((tm, D_MODEL), lambda i, k: (i, 0))
        out_shape = jax.ShapeDtypeStruct((r, D_MODEL), F32)
    return pl.pallas_call(
        functools.partial(_ffn_kernel, final=final),
        grid=(r // tm, D_FF // tf),
        in_specs=[
            pl.BlockSpec((tm, D_MODEL), lambda i, k: (i, 0)),
            pl.BlockSpec((tm, D_MODEL), lambda i, k: (i, 0)),
            pl.BlockSpec((1, N_MOD, SUBLANES, D_MODEL), lambda i, k: (layer, 0, 0, 0)),
            pl.BlockSpec((1, 1, D_MODEL, 2 * tf), lambda i, k: (layer, k, 0, 0)),
            pl.BlockSpec((1, tf, D_MODEL), lambda i, k: (layer, k, 0)),
            pl.BlockSpec((1, D_MODEL), lambda i, k: (0, 0)),
        ],
        out_specs=out_spec,
        out_shape=out_shape,
        scratch_shapes=[pltpu.VMEM((tm, D_MODEL), F32)],
        compiler_params=_params("parallel", "arbitrary"),
        name="ffn",
    )(h2, x1, mod, w_gate_up, w_down, g_final)


def _s5_build_kernel(pw_ref, bb_ref, cs_ref, bc_ref, cc_ref, toep_ref):
    row_group = lax.broadcasted_iota(jnp.int32, (LANES, LANES), 0) // SSM_H
    lane = lax.broadcasted_iota(jnp.int32, (LANES, LANES), 1)
    row_group_s = lax.broadcasted_iota(jnp.int32, (LANES, SSM_STATE), 0) // SSM_H

    def per_row(p):
        out = jnp.zeros((LANES, SSM_STATE), F32)
        for g in range(GROUPS_PER_BLOCK):
            out = jnp.where(row_group_s == g, p[g:g + 1, :], out)
        return out

    dup_state = (lax.broadcasted_iota(jnp.int32, (SSM_STATE, LANES), 1) % SSM_STATE
                 == lax.broadcasted_iota(jnp.int32, (SSM_STATE, LANES), 0)).astype(BF16)
    keep = [(row_group // 2 == q) & (lane // SSM_STATE == row_group % 2) for q in range(PAIRS)]
    over_states = (((1,), (1,)), ((), ()))
    taps = {}
    for d in range(N_DIR):
        b_re, b_im = bb_ref[0, d, 0, 0], bb_ref[0, d, 1, 0]
        c_re, c_im = cs_ref[0, d, 0, 0], cs_ref[0, d, 1, 0]
        for i in range(T_BLK):
            pr, pi = per_row(pw_ref[0, d, 0, i, 0, 0]), per_row(pw_ref[0, d, 0, i, 1, 0])
            qr, qi = per_row(pw_ref[0, d, 1, i, 0, 0]), per_row(pw_ref[0, d, 1, i, 1, 0])
            src_b = (b_re * pr - b_im * pi, b_re * pi + b_im * pr)
            src_c = (c_re * qr - c_im * qi, -(c_re * qi + c_im * qr))
            lag = T_BLK - 1 - i if d == 0 else i
            taps[d, lag] = (
                lax.dot_general(src_b[0], c_re, over_states, precision=lax.Precision.HIGHEST,
                                preferred_element_type=F32)
                - lax.dot_general(src_b[1], c_im, over_states, precision=lax.Precision.HIGHEST,
                                  preferred_element_type=F32))
            for r in range(2):
                wide_b = jnp.dot(src_b[r].astype(BF16), dup_state, preferred_element_type=F32)
                wide_c = jnp.dot(src_c[r].astype(BF16), dup_state, preferred_element_type=F32)
                for q in range(PAIRS):
                    c0 = d * STATE_LANES + r * (STATE_LANES // 2) + q * LANES
                    bc_ref[0, 0, i * LANES:(i + 1) * LANES, c0:c0 + LANES] = (
                        jnp.where(keep[q], wide_b, 0.0).astype(BF16))
                    cc_ref[0, 0, c0:c0 + LANES, i * LANES:(i + 1) * LANES] = (
                        jnp.where(keep[q], wide_c, 0.0).T.astype(BF16))
    diag = row_group == lane // SSM_H
    for i_in in range(T_BLK):
        for i_out in range(T_BLK):
            lag = i_out - i_in
            tap = taps[0, lag] if lag > 0 else taps[1, -lag] if lag < 0 else taps[0, 0] + taps[1, 0]
            toep_ref[0, 0, i_in * LANES:(i_in + 1) * LANES, i_out * LANES:(i_out + 1) * LANES] = (
                jnp.where(diag, tap, 0.0).astype(BF16))


def _cpow(lr, li, dt, k):
    m = jnp.exp(lr * dt * k)
    return m * jnp.cos(li * dt * k), m * jnp.sin(li * dt * k)


def _s5_sources(lam_re, lam_im, log_dt, b_re, b_im, c_re, c_im):
    t = T_BLK
    nb = N_GROUP_BLOCKS
    lr = jnp.minimum(lam_re, LAM_RE_MAX)
    li = lam_im
    dt = jnp.exp(log_dt)[..., None]
    ar, ai = _cpow(lr, li, dt, 1.0)
    den = lr * lr + li * li
    qr = (((ar - 1.0) * lr + ai * li) / den)[:, :, None, :]
    qi = ((ai * lr - (ar - 1.0) * li) / den)[:, :, None, :]
    b_re_t = b_re.transpose(0, 1, 3, 2)
    b_im_t = b_im.transpose(0, 1, 3, 2)
    bbr = qr * b_re_t - qi * b_im_t
    bbi = qr * b_im_t + qi * b_re_t

    steps = jnp.arange(t, dtype=F32)
    e_in = jnp.stack([t - 1.0 - steps, steps], axis=0)[:, :, None, None]
    e_out = jnp.stack([steps + 1.0, t - steps], axis=0)[:, :, None, None]
    lr4, li4, dt4 = lr[:, None], li[:, None], dt[:, None]
    pw = jnp.stack([jnp.stack(_cpow(lr4, li4, dt4, e_in), axis=2),
                    jnp.stack(_cpow(lr4, li4, dt4, e_out), axis=2)], axis=1)
    pw = pw.reshape(N_DIR, 2, t, 2, nb, GROUPS_PER_BLOCK, SSM_STATE)
    bbt = jnp.stack([bbr, bbi], axis=1).reshape(N_DIR, 2, nb, LANES, SSM_STATE)
    cs = jnp.stack([c_re, c_im], axis=1).reshape(N_DIR, 2, nb, LANES, SSM_STATE)

    atr, ati = _cpow(lr, li, dt, float(t))
    a = jnp.stack([atr, ati], axis=0)
    a = a.reshape(2, N_DIR, nb, PAIRS * LANES).transpose(2, 0, 1, 3).reshape(nb, 2, 1, N_DIR * PAIRS * LANES)
    a = jnp.broadcast_to(a, (nb, 2, SUBLANES, N_DIR * PAIRS * LANES))
    return pw, bbt, cs, a


def _s5_matrices(lam_re, lam_im, log_dt, b_re, b_im, c_re, c_im):
    t = T_BLK
    nb = N_GROUP_BLOCKS
    pw, bbt, cs, a = jax.vmap(_s5_sources)(lam_re, lam_im, log_dt, b_re, b_im, c_re, c_im)
    coef_spec = pl.BlockSpec((1, N_DIR, 2, 1, LANES, SSM_STATE), lambda l, g: (l, 0, 0, g, 0, 0))
    shape = lambda r, c: jax.ShapeDtypeStruct((DEPTH, nb, r, c), BF16)
    spec = lambda r, c: pl.BlockSpec((1, 1, r, c), lambda l, g: (l, g, 0, 0))
    bc, cc, toep = pl.pallas_call(
        _s5_build_kernel,
        grid=(DEPTH, nb),
        in_specs=[
            pl.BlockSpec((1, N_DIR, 2, t, 2, 1, GROUPS_PER_BLOCK, SSM_STATE), lambda l, g: (l, 0, 0, 0, 0, g, 0, 0)),
            coef_spec,
            coef_spec,
        ],
        out_specs=[spec(t * LANES, N_DIR * STATE_LANES), spec(N_DIR * STATE_LANES, t * LANES),
                   spec(t * LANES, t * LANES)],
        out_shape=[shape(t * LANES, N_DIR * STATE_LANES), shape(N_DIR * STATE_LANES, t * LANES),
                   shape(t * LANES, t * LANES)],
        compiler_params=_params("parallel", "parallel"),
        name="s5_build",
    )(pw, bbt, cs)
    return bc, cc, toep, a


def _init_states(st):
    b = st.shape[0]
    s = st.reshape(b, N_DIR, 2, N_GROUP_BLOCKS, STATE_LANES // 2)
    return s.transpose(3, 1, 0, 2, 4).reshape(N_GROUP_BLOCKS, N_DIR, b, STATE_LANES)


def _final_states(fin):
    b = fin.shape[2]
    s = fin.reshape(N_GROUP_BLOCKS, N_DIR, b, 2, STATE_LANES // 2)
    return s.transpose(2, 1, 3, 0, 4).reshape(b, N_DIR, 2, SSM_GROUPS, SSM_STATE)


def kernel(x_prompt, x_sample, state_ssm, c, c_ctx, w_ada, b_ada, g_mix, w_in, ssm_lam_re, ssm_lam_im, ssm_log_dt, ssm_b_re, ssm_b_im, ssm_c_re, ssm_c_im, ssm_d, w_glu, conv_w, conv_b, w_out, g_ffn, w_gate, w_up, w_down, g_final):
    n_ctx, l_ctx, _ = x_prompt.shape
    n_dec, l_dec, _ = x_sample.shape

    cond = jnp.zeros((16, D_MODEL), F32).at[0].set(c_ctx).at[1:1 + n_dec].set(c)
    mod = _modulation(cond, w_ada, b_ada).reshape(DEPTH, 16, N_MOD, D_MODEL)
    mod_ctx = jnp.broadcast_to(mod[:, 0, :, None, :], (DEPTH, N_MOD, SUBLANES, D_MODEL))
    mod_dec = mod[:, 1:1 + n_dec].transpose(0, 2, 1, 3)

    w_in_t = _cast_col_tiles(w_in, TN_INPROJ, parts=4)
    w_glu_b = _cast_rows(w_glu, 512)
    w_out_b = _cast_rows(w_out, 512)
    w_gate_up_b = _cast_pair_col_tiles(w_gate, w_up, TF_FFN)
    w_down_b = _cast_rows(w_down, TF_FFN)
    g_mix3 = g_mix[:, None, :]
    g_ffn3 = g_ffn[:, None, :]
    conv_b3 = conv_b[:, None, :]
    ssm_d3 = ssm_d[:, None, :]

    groups = [
        dict(x=x_prompt, batch=n_ctx, seg=l_ctx, mod=mod_ctx, s0=None),
        dict(x=x_sample, batch=n_dec, seg=GRID_W, mod=mod_dec, s0=state_ssm.astype(F32)),
    ]

    bc, cc, toep, a = _s5_matrices(ssm_lam_re, ssm_lam_im, ssm_log_dt, ssm_b_re, ssm_b_im, ssm_c_re, ssm_c_im)
    finals = []
    for l in range(DEPTH):
        for grp in groups:
            batch = grp["batch"]
            s0 = None if grp["s0"] is None else _init_states(grp["s0"][:, l])
            if l == 0:
                x, u, gb, zc = _inproj(grp["x"], grp["mod"], g_mix3, w_in_t, l, batch_major=True)
            else:
                x = grp["x"]
                u, gb, zc = _inproj(x, grp["mod"], g_mix3, w_in_t, l, batch_major=False)
            y, fin = _s5(u, bc, cc, toep, a, ssm_d3, s0, batch, l)
            x1, h2 = _mixout(y, gb, zc, x, grp["mod"], w_glu_b, w_out_b, conv_w, conv_b3, g_ffn3,
                             batch, grp["seg"], l)
            grp["x"] = _ffn(h2, x1, grp["mod"], w_gate_up_b, w_down_b, g_final[None], l, batch,
                            final=(l == DEPTH - 1))
            if grp["s0"] is None:
                finals.append(_final_states(fin))

    new_state = jnp.stack(finals, axis=1).astype(x_prompt.dtype)
    return (groups[0]["x"], groups[1]["x"], new_state)
```

```python
import functools

import jax
import jax.numpy as jnp
from jax import lax
from jax.experimental import pallas as pl
from jax.experimental.pallas import tpu as pltpu

D_MODEL = 2048
DEPTH = 2
GRID_W = 64
D_SSM = 1024
D_CONV = 1024
SSM_H = 16
SSM_GROUPS = 64
SSM_STATE = 64
N_DIR = 2
D_FF = 5632
N_MOD = 6
EPS = 1e-6
LAM_RE_MAX = -1e-4

SUBLANES = 8
LANES = 128
GROUPS_PER_BLOCK = LANES // SSM_H
N_GROUP_BLOCKS = SSM_GROUPS // GROUPS_PER_BLOCK
PAIRS = GROUPS_PER_BLOCK // 2
STATE_LANES = GROUPS_PER_BLOCK * 2 * SSM_STATE
SCAN_ROWS = 1024
HALO_ROWS = 16
T_BLK = 4
VMEM_LIMIT = 56 * 1024 * 1024

TM_INPROJ = 1024
TM_MIXOUT = 512
TM_FFN = 512
TN_INPROJ = 1024
TF_FFN = 512

F32 = jnp.float32
BF16 = jnp.bfloat16


def _params(*sem):
    return pltpu.CompilerParams(dimension_semantics=sem, vmem_limit_bytes=VMEM_LIMIT)


def _resident(shape, index_map):
    return pl.BlockSpec(shape, index_map, pipeline_mode=pl.Buffered(1))


def _rows8(a):
    return a.reshape(a.shape[0] // SUBLANES, SUBLANES, a.shape[1])


def _norm_mod(x, gain, scale, shift):
    ms = jnp.mean(x * x, axis=-1, keepdims=True)
    y = x * lax.rsqrt(ms + EPS) * gain
    h = _rows8(y) * (1.0 + scale)[None] + shift[None]
    return h.reshape(x.shape)


def _cast_kernel(w_ref, o_ref):
    o_ref[...] = w_ref[...].astype(BF16).reshape(o_ref.shape)


def _cast_col_tiles(w, tn, parts=1):
    _, k, n = w.shape
    nt = n // tn
    q = tn // parts
    return pl.pallas_call(
        _cast_kernel,
        grid=(DEPTH, nt, parts),
        in_specs=[pl.BlockSpec((1, k, q), lambda l, j, p: (l, 0, p * nt + j))],
        out_specs=pl.BlockSpec((1, 1, k, q), lambda l, j, p: (l, j, 0, p)),
        out_shape=jax.ShapeDtypeStruct((DEPTH, nt, k, tn), BF16),
        compiler_params=_params("parallel", "parallel", "parallel"),
        name="cast_col_tiles",
    )(w)


def _cast_pair_kernel(a_ref, b_ref, o_ref):
    half = a_ref.shape[2] // 2
    for s in range(2):
        o_ref[0, 0, :, (2 * s) * half:(2 * s + 1) * half] = a_ref[0, :, s * half:(s + 1) * half].astype(BF16)
        o_ref[0, 0, :, (2 * s + 1) * half:(2 * s + 2) * half] = b_ref[0, :, s * half:(s + 1) * half].astype(BF16)


def _cast_pair_col_tiles(wa, wb, tn):
    _, k, n = wa.shape
    spec = pl.BlockSpec((1, k, tn), lambda l, j: (l, 0, j))
    return pl.pallas_call(
        _cast_pair_kernel,
        grid=(DEPTH, n // tn),
        in_specs=[spec, spec],
        out_specs=pl.BlockSpec((1, 1, k, 2 * tn), lambda l, j: (l, j, 0, 0)),
        out_shape=jax.ShapeDtypeStruct((DEPTH, n // tn, k, 2 * tn), BF16),
        compiler_params=_params("parallel", "parallel"),
        name="cast_pair_col_tiles",
    )(wa, wb)


def _cast_rows(w, tk):
    _, k, n = w.shape
    return pl.pallas_call(
        _cast_kernel,
        grid=(DEPTH, k // tk),
        in_specs=[pl.BlockSpec((1, tk, n), lambda l, i: (l, i, 0))],
        out_specs=pl.BlockSpec((1, tk, n), lambda l, i: (l, i, 0)),
        out_shape=jax.ShapeDtypeStruct((DEPTH, k, n), BF16),
        compiler_params=_params("parallel", "parallel"),
        name="cast_rows",
    )(w)


def _mod_kernel(cond_ref, w_ref, b_ref, o_ref):
    s = jax.nn.silu(cond_ref[...]).astype(BF16)
    o_ref[0] = jnp.dot(s, w_ref[0].astype(BF16), preferred_element_type=F32) + b_ref[0]


def _modulation(cond, w_ada, b_ada):
    tn = 1024
    n = N_MOD * D_MODEL
    return pl.pallas_call(
        _mod_kernel,
        grid=(DEPTH, n // tn),
        in_specs=[
            pl.BlockSpec((16, D_MODEL), lambda l, j: (0, 0)),
            pl.BlockSpec((1, D_MODEL, tn), lambda l, j: (l, 0, j)),
            pl.BlockSpec((1, 1, tn), lambda l, j: (l, 0, j)),
        ],
        out_specs=pl.BlockSpec((1, 16, tn), lambda l, j: (l, 0, j)),
        out_shape=jax.ShapeDtypeStruct((DEPTH, 16, n), F32),
        compiler_params=_params("parallel", "parallel"),
        name="adaln_modulation",
    )(cond, w_ada, b_ada.reshape(DEPTH, 1, n))


def _inproj_kernel(x_ref, mod_ref, g_ref, w_ref, *out_refs, batch_major):
    if batch_major:
        xt_ref, u_ref, gb_ref, zc_ref, h_even, h_odd = out_refs
    else:
        u_ref, gb_ref, zc_ref, h_even, h_odd = out_refs
    i = pl.program_id(0)
    j = pl.program_id(1)
    tq = h_even.shape[0] // pl.num_programs(1)

    def prepare(h_next):
        if batch_major:
            x = jnp.concatenate([x_ref[:, t, :] for t in range(x_ref.shape[1])], axis=0)
            xt_ref[...] = x
        else:
            x = x_ref[...]
        h = _norm_mod(x, g_ref[0], mod_ref[0, 1], mod_ref[0, 0])
        h_next[pl.ds(pl.multiple_of(j * tq, tq), tq), :] = h.astype(BF16)

    def multiply(h_cur):
        z = jnp.dot(h_cur[...], w_ref[0, 0], preferred_element_type=F32)
        q = z.shape[1] // 4
        u_ref[...] = z[:, :q]
        gb_ref[...] = z[:, q:2 * q].astype(gb_ref.dtype)
        zc_ref[...] = (z[:, 2 * q:3 * q] * z[:, 3 * q:]).astype(zc_ref.dtype)

    n = pl.num_programs(0) - 1
    bufs = (h_even, h_odd)

    @pl.when(i == 0)
    def _():
        prepare(h_even)

    for parity in range(2):
        @pl.when((i > 0) & (i < n) & (i % 2 == parity))
        def _():
            prepare(bufs[parity])
            multiply(bufs[1 - parity])

        @pl.when((i == n) & (i % 2 == parity))
        def _():
            multiply(bufs[1 - parity])


def _inproj(x, mod, g_mix, w_in_tiled, layer, batch_major):
    tm, tn = TM_INPROJ, TN_INPROJ
    q = tn // 4
    nj = 4 * D_SSM // tn
    tq = tm // nj
    r = x.shape[0] * x.shape[1] if batch_major else x.shape[0]
    n = r // tm
    quarter = lambda i, j: jnp.minimum(i * nj + j, n * nj - 1)
    if batch_major:
        b = x.shape[0]
        x_spec = pl.BlockSpec((b, tq // b, D_MODEL), lambda i, j: (0, quarter(i, j), 0))
    else:
        x_spec = pl.BlockSpec((tq, D_MODEL), lambda i, j: (quarter(i, j), 0))
    out = jax.ShapeDtypeStruct((r, D_SSM), F32)
    out_specs = [pl.BlockSpec((tm, q), lambda i, j: (jnp.maximum(i - 1, 0), jnp.where(i == 0, 0, j)))] * 3
    gate = jax.ShapeDtypeStruct((r, D_CONV), BF16)
    out_shape = [out, gate, gate]
    if batch_major:
        out_specs = [pl.BlockSpec((tq, D_MODEL), lambda i, j: (quarter(i, j), 0))] + out_specs
        out_shape = [jax.ShapeDtypeStruct((r, D_MODEL), F32)] + out_shape
    return pl.pallas_call(
        functools.partial(_inproj_kernel, batch_major=batch_major),
        grid=(n + 1, nj),
        in_specs=[
            x_spec,
            pl.BlockSpec((1, N_MOD, SUBLANES, D_MODEL), lambda i, j: (layer, 0, 0, 0)),
            pl.BlockSpec((1, 1, D_MODEL), lambda i, j: (layer, 0, 0)),
            pl.BlockSpec((1, 1, D_MODEL, tn), lambda i, j: (layer, j, 0, 0)),
        ],
        out_specs=out_specs,
        out_shape=out_shape,
        scratch_shapes=[pltpu.VMEM((tm, D_MODEL), BF16), pltpu.VMEM((tm, D_MODEL), BF16)],
        compiler_params=_params("arbitrary", "arbitrary"),
        name="inproj",
    )(x, mod, g_mix, w_in_tiled)


def _s5_kernel(*refs, batch, has_init):
    if has_init:
        u_ref, bc_ref, cc_ref, toep_ref, a_ref, d_ref, s0_ref, y_ref, fin_ref, sf, sb, xs = refs
    else:
        u_ref, bc_ref, cc_ref, toep_ref, a_ref, d_ref, y_ref, fin_ref, sf, sb, xs = refs
        s0_ref = None
    rows = u_ref.shape[0]
    rc = SCAN_ROWS
    n_chunks = rows // rc
    nblk = rc // (batch * T_BLK)
    rblk = nblk * batch
    halves = batch // SUBLANES
    tl = T_BLK * LANES

    y_ref[...] = d_ref[0] * u_ref[...]
    if has_init:
        xs[...] = s0_ref[0]
    else:
        xs[...] = jnp.zeros_like(xs)

    def gather(r0):
        uc = u_ref[pl.ds(r0, rc), :].reshape(nblk, T_BLK, batch, LANES)
        return jnp.concatenate([uc[:, i].reshape(rblk, LANES) for i in range(T_BLK)], axis=1).astype(BF16)

    def scatter_add(r0, yb):
        parts = [yb[:, i * LANES:(i + 1) * LANES].reshape(nblk, 1, batch, LANES) for i in range(T_BLK)]
        y_ref[pl.ds(r0, rc), :] += jnp.concatenate(parts, axis=1).reshape(rc, LANES)

    def chunk(ci, carry):
        rf = pl.multiple_of(ci * rc, rc)
        rb = pl.multiple_of((n_chunks - 1 - ci) * rc, rc)
        lf = gather(rf)
        lb = gather(rb)
        sf[...] = jnp.dot(lf, bc_ref[0, 0, :, :STATE_LANES], preferred_element_type=F32)
        sb[...] = jnp.dot(lb, bc_ref[0, 0, :, STATE_LANES:], preferred_element_type=F32)
        for half in range(halves):
            hs = slice(half * SUBLANES, (half + 1) * SUBLANES)
            for pair in range(PAIRS):
                re = slice(pair * LANES, (pair + 1) * LANES)
                im = slice(STATE_LANES // 2 + pair * LANES, STATE_LANES // 2 + (pair + 1) * LANES)
                for d, s in ((0, sf), (1, sb)):
                    al = slice(d * PAIRS * LANES + pair * LANES, d * PAIRS * LANES + (pair + 1) * LANES)
                    ar = a_ref[0, 0, 0, :, al]
                    ai = a_ref[0, 0, 1, :, al]
                    xr = xs[d, hs, re]
                    xi = xs[d, hs, im]
                    for t in range(nblk):
                        blk = t if d == 0 else nblk - 1 - t
                        rs = slice(blk * batch + half * SUBLANES, blk * batch + (half + 1) * SUBLANES)
                        sr = s[rs, re]
                        si = s[rs, im]
                        s[rs, re] = xr
                        s[rs, im] = xi
                        xr, xi = ar * xr - ai * xi + sr, ar * xi + ai * xr + si
                    xs[d, hs, re] = xr
                    xs[d, hs, im] = xi
        yf = jnp.dot(sf[...].astype(BF16), cc_ref[0, 0, :STATE_LANES, :], preferred_element_type=F32)
        yf += jnp.dot(lf, toep_ref[0, 0], preferred_element_type=F32)
        yb = jnp.dot(sb[...].astype(BF16), cc_ref[0, 0, STATE_LANES:, :], preferred_element_type=F32)
        scatter_add(rf, yf)
        scatter_add(rb, yb)
        return carry

    lax.fori_loop(0, n_chunks, chunk, 0)
    fin_ref[0] = xs[...]


def _s5(u, bc, cc, toep, a, d_skip, s0, batch, layer):
    r = u.shape[0]
    rblk = SCAN_ROWS // T_BLK
    tl = T_BLK * LANES
    has_init = s0 is not None
    in_specs = [
        pl.BlockSpec((r, LANES), lambda g: (0, g)),
        pl.BlockSpec((1, 1, tl, N_DIR * STATE_LANES), lambda g: (layer, g, 0, 0)),
        pl.BlockSpec((1, 1, N_DIR * STATE_LANES, tl), lambda g: (layer, g, 0, 0)),
        pl.BlockSpec((1, 1, tl, tl), lambda g: (layer, g, 0, 0)),
        pl.BlockSpec((1, 1, 2, SUBLANES, N_DIR * PAIRS * LANES), lambda g: (layer, g, 0, 0, 0)),
        pl.BlockSpec((1, 1, LANES), lambda g: (layer, 0, g)),
    ]
    args = [u, bc, cc, toep, a, d_skip]
    if has_init:
        in_specs.append(pl.BlockSpec((1, N_DIR, batch, STATE_LANES), lambda g: (g, 0, 0, 0)))
        args.append(s0)
    return pl.pallas_call(
        functools.partial(_s5_kernel, batch=batch, has_init=has_init),
        grid=(N_GROUP_BLOCKS,),
        in_specs=in_specs,
        out_specs=[
            pl.BlockSpec((r, LANES), lambda g: (0, g)),
            pl.BlockSpec((1, N_DIR, batch, STATE_LANES), lambda g: (g, 0, 0, 0)),
        ],
        out_shape=[
            jax.ShapeDtypeStruct((r, D_SSM), F32),
            jax.ShapeDtypeStruct((N_GROUP_BLOCKS, N_DIR, batch, STATE_LANES), F32),
        ],
        scratch_shapes=[
            pltpu.VMEM((rblk, STATE_LANES), F32),
            pltpu.VMEM((rblk, STATE_LANES), F32),
            pltpu.VMEM((N_DIR, batch, STATE_LANES), F32),
        ],
        compiler_params=_params("parallel"),
        name="s5_mixer",
    )(*args)


def _mixout_kernel(y_ref, gb_ref, zc_ref, zp_ref, zn_ref, x_ref, mod_ref, wglu_ref, wout_ref,
                   cw_ref, cb_ref, g_ref, x1_ref, h2_ref, *, batch, seg):
    tm = x_ref.shape[0]
    tl = tm // batch
    l0 = pl.program_id(0) * tl
    prev_ok = (l0 % seg != 0).astype(F32)
    next_ok = ((l0 + tl) % seg != 0).astype(F32)

    y = jax.nn.gelu(y_ref[...])
    y_ssm = y * jax.nn.sigmoid(jnp.dot(y.astype(BF16), wglu_ref[0], preferred_element_type=F32))

    zc = zc_ref[...].astype(F32)
    zp = zp_ref[...].astype(F32)[HALO_ROWS - batch:, :]
    zn = zn_ref[...].astype(F32)[:batch, :]
    prev = jnp.concatenate([zp * prev_ok, zc[:tm - batch]], axis=0)
    nxt = jnp.concatenate([zc[batch:], zn * next_ok], axis=0)
    conv = prev * cw_ref[0, 0:1, :] + zc * cw_ref[0, 1:2, :] + nxt * cw_ref[0, 2:3, :] + cb_ref[0]
    y_conv = gb_ref[...].astype(F32) * conv

    out = jnp.dot(y_conv.astype(BF16), wout_ref[0, D_SSM:, :], preferred_element_type=F32)
    out += jnp.dot(y_ssm.astype(BF16), wout_ref[0, :D_SSM, :], preferred_element_type=F32)
    x1 = (_rows8(x_ref[...]) + mod_ref[0, 2][None] * _rows8(out)).reshape(tm, D_MODEL)
    x1_ref[...] = x1
    h2_ref[...] = _norm_mod(x1, g_ref[0], mod_ref[0, 4], mod_ref[0, 3]).astype(BF16)


def _mixout(y, gb, zc, x, mod, w_glu, w_out, conv_w, conv_b, g_ffn, batch, seg, layer):
    r = x.shape[0]
    tm = TM_MIXOUT
    assert seg % (tm // batch) == 0
    hb = tm // HALO_ROWS
    n_hb = r // HALO_ROWS
    row = lambda i: (i, 0)
    lay = lambda i: (layer, 0, 0)
    return pl.pallas_call(
        functools.partial(_mixout_kernel, batch=batch, seg=seg),
        grid=(r // tm,),
        in_specs=[
            pl.BlockSpec((tm, D_SSM), row),
            pl.BlockSpec((tm, D_CONV), row),
            pl.BlockSpec((tm, D_CONV), row),
            pl.BlockSpec((HALO_ROWS, D_CONV), lambda i: (jnp.maximum(i * hb - 1, 0), 0)),
            pl.BlockSpec((HALO_ROWS, D_CONV), lambda i: (jnp.minimum((i + 1) * hb, n_hb - 1), 0)),
            pl.BlockSpec((tm, D_MODEL), row),
            _resident((1, N_MOD, SUBLANES, D_MODEL), lambda i: (layer, 0, 0, 0)),
            _resident((1, D_SSM, D_SSM), lay),
            _resident((1, D_MODEL, D_MODEL), lay),
            _resident((1, 3, D_CONV), lay),
            _resident((1, 1, D_CONV), lay),
            _resident((1, 1, D_MODEL), lay),
        ],
        out_specs=[pl.BlockSpec((tm, D_MODEL), row), pl.BlockSpec((tm, D_MODEL), row)],
        out_shape=[jax.ShapeDtypeStruct((r, D_MODEL), F32), jax.ShapeDtypeStruct((r, D_MODEL), BF16)],
        compiler_params=_params("parallel"),
        name="mixer_out",
    )(y, gb, zc, zc, zc, x, mod, w_glu, w_out, conv_w, conv_b, g_ffn)


def _ffn_kernel(h_ref, x_ref, mod_ref, wgu_ref, wd_ref, gf_ref, o_ref, acc, *, final):
    k = pl.program_id(1)

    @pl.when(k == 0)
    def _():
        acc[...] = jnp.zeros_like(acc)

    gate_up = jnp.dot(h_ref[...], wgu_ref[0, 0], preferred_element_type=F32)
    hw = gate_up.shape[1] // 4
    act = jnp.concatenate(
        [jax.nn.silu(gate_up[:, 2 * s * hw:(2 * s + 1) * hw]) * gate_up[:, (2 * s + 1) * hw:(2 * s + 2) * hw]
         for s in range(2)], axis=1).astype(BF16)
    acc[...] += jnp.dot(act, wd_ref[0], preferred_element_type=F32)

    @pl.when(k == pl.num_programs(1) - 1)
    def _():
        x2 = (_rows8(x_ref[...]) + mod_ref[0, 5][None] * _rows8(acc[...])).reshape(x_ref.shape)
        if final:
            ms = jnp.mean(x2 * x2, axis=-1, keepdims=True)
            x2 = x2 * lax.rsqrt(ms + EPS) * gf_ref[...]
            b = o_ref.shape[0]
            for t in range(o_ref.shape[1]):
                o_ref[:, t, :] = x2[t * b:(t + 1) * b, :]
        else:
            o_ref[...] = x2


def _ffn(h2, x1, mod, w_gate_up, w_down, g_final, layer, batch, final):
    r = x1.shape[0]
    tm, tf = TM_FFN, TF_FFN
    if final:
        out_spec = pl.BlockSpec((batch, tm // batch, D_MODEL), lambda i, k: (0, i, 0))
        out_shape = jax.ShapeDtypeStruct((batch, r // batch, D_MODEL), F32)
    else:
        out_spec = pl.BlockSpec((tm, D_MODEL), lambda i, k: (i, 0))
        out_shape = jax.ShapeDtypeStruct((r, D_MODEL), F32)
    return pl.pallas_call(
        functools.partial(_ffn_kernel, final=final),
        grid=(r // tm, D_FF // tf),
        in_specs=[
            pl.BlockSpec((tm, D_MODEL), lambda i, k: (i, 0)),
            pl.BlockSpec((tm, D_MODEL), lambda i, k: (i, 0)),
            pl.BlockSpec((1, N_MOD, SUBLANES, D_MODEL), lambda i, k: (layer, 0, 0, 0)),
            pl.BlockSpec((1, 1, D_MODEL, 2 * tf), lambda i, k: (layer, k, 0, 0)),
            pl.BlockSpec((1, tf, D_MODEL), lambda i, k: (layer, k, 0)),
            pl.BlockSpec((1, D_MODEL), lambda i, k: (0, 0)),
        ],
        out_specs=out_spec,
        out_shape=out_shape,
        scratch_shapes=[pltpu.VMEM((tm, D_MODEL), F32)],
        compiler_params=_params("parallel", "arbitrary"),
        name="ffn",
    )(h2, x1, mod, w_gate_up, w_down, g_final)


def _s5_build_kernel(pw_ref, bb_ref, cs_ref, bc_ref, cc_ref, toep_ref):
    row_group = lax.broadcasted_iota(jnp.int32, (LANES, LANES), 0) // SSM_H
    lane = lax.broadcasted_iota(jnp.int32, (LANES, LANES), 1)
    row_group_s = lax.broadcasted_iota(jnp.int32, (LANES, SSM_STATE), 0) // SSM_H

    def per_row(p):
        out = jnp.zeros((LANES, SSM_STATE), F32)
        for g in range(GROUPS_PER_BLOCK):
            out = jnp.where(row_group_s == g, p[g:g + 1, :], out)
        return out

    dup_state = (lax.broadcasted_iota(jnp.int32, (SSM_STATE, LANES), 1) % SSM_STATE
                 == lax.broadcasted_iota(jnp.int32, (SSM_STATE, LANES), 0)).astype(BF16)
    keep = [(row_group // 2 == q) & (lane // SSM_STATE == row_group % 2) for q in range(PAIRS)]
    over_states = (((1,), (1,)), ((), ()))
    taps = {}
    for d in range(N_DIR):
        b_re, b_im = bb_ref[0, d, 0, 0], bb_ref[0, d, 1, 0]
        c_re, c_im = cs_ref[0, d, 0, 0], cs_ref[0, d, 1, 0]
        for i in range(T_BLK):
            pr, pi = per_row(pw_ref[0, d, 0, i, 0, 0]), per_row(pw_ref[0, d, 0, i, 1, 0])
            qr, qi = per_row(pw_ref[0, d, 1, i, 0, 0]), per_row(pw_ref[0, d, 1, i, 1, 0])
            src_b = (b_re * pr - b_im * pi, b_re * pi + b_im * pr)
            src_c = (c_re * qr - c_im * qi, -(c_re * qi + c_im * qr))
            lag = T_BLK - 1 - i if d == 0 else i
            taps[d, lag] = (
                lax.dot_general(src_b[0], c_re, over_states, precision=lax.Precision.HIGHEST,
                                preferred_element_type=F32)
                - lax.dot_general(src_b[1], c_im, over_states, precision=lax.Precision.HIGHEST,
                                  preferred_element_type=F32))
            for r in range(2):
                wide_b = jnp.dot(src_b[r].astype(BF16), dup_state, preferred_element_type=F32)
                wide_c = jnp.dot(src_c[r].astype(BF16), dup_state, preferred_element_type=F32)
                for q in range(PAIRS):
                    c0 = d * STATE_LANES + r * (STATE_LANES // 2) + q * LANES
                    bc_ref[0, 0, i * LANES:(i + 1) * LANES, c0:c0 + LANES] = (
                        jnp.where(keep[q], wide_b, 0.0).astype(BF16))
                    cc_ref[0, 0, c0:c0 + LANES, i * LANES:(i + 1) * LANES] = (
                        jnp.where(keep[q], wide_c, 0.0).T.astype(BF16))
    diag = row_group == lane // SSM_H
    for i_in in range(T_BLK):
        for i_out in range(T_BLK):
            lag = i_out - i_in
            tap = taps[0, lag] if lag > 0 else taps[1, -lag] if lag < 0 else taps[0, 0] + taps[1, 0]
            toep_ref[0, 0, i_in * LANES:(i_in + 1) * LANES, i_out * LANES:(i_out + 1) * LANES] = (
                jnp.where(diag, tap, 0.0).astype(BF16))


def _cpow(lr, li, dt, k):
    m = jnp.exp(lr * dt * k)
    return m * jnp.cos(li * dt * k), m * jnp.sin(li * dt * k)


def _s5_sources(lam_re, lam_im, log_dt, b_re, b_im, c_re, c_im):
    t = T_BLK
    nb = N_GROUP_BLOCKS
    lr = jnp.minimum(lam_re, LAM_RE_MAX)
    li = lam_im
    dt = jnp.exp(log_dt)[..., None]
    ar, ai = _cpow(lr, li, dt, 1.0)
    den = lr * lr + li * li
    qr = (((ar - 1.0) * lr + ai * li) / den)[:, :, None, :]
    qi = ((ai * lr - (ar - 1.0) * li) / den)[:, :, None, :]
    b_re_t = b_re.transpose(0, 1, 3, 2)
    b_im_t = b_im.transpose(0, 1, 3, 2)
    bbr = qr * b_re_t - qi * b_im_t
    bbi = qr * b_im_t + qi * b_re_t

    steps = jnp.arange(t, dtype=F32)
    e_in = jnp.stack([t - 1.0 - steps, steps], axis=0)[:, :, None, None]
    e_out = jnp.stack([steps + 1.0, t - steps], axis=0)[:, :, None, None]
    lr4, li4, dt4 = lr[:, None], li[:, None], dt[:, None]
    pw = jnp.stack([jnp.stack(_cpow(lr4, li4, dt4, e_in), axis=2),
                    jnp.stack(_cpow(lr4, li4, dt4, e_out), axis=2)], axis=1)
    pw = pw.reshape(N_DIR, 2, t, 2, nb, GROUPS_PER_BLOCK, SSM_STATE)
    bbt = jnp.stack([bbr, bbi], axis=1).reshape(N_DIR, 2, nb, LANES, SSM_STATE)
    cs = jnp.stack([c_re, c_im], axis=1).reshape(N_DIR, 2, nb, LANES, SSM_STATE)

    atr, ati = _cpow(lr, li, dt, float(t))
    a = jnp.stack([atr, ati], axis=0)
    a = a.reshape(2, N_DIR, nb, PAIRS * LANES).transpose(2, 0, 1, 3).reshape(nb, 2, 1, N_DIR * PAIRS * LANES)
    a = jnp.broadcast_to(a, (nb, 2, SUBLANES, N_DIR * PAIRS * LANES))
    return pw, bbt, cs, a


def _s5_matrices(lam_re, lam_im, log_dt, b_re, b_im, c_re, c_im):
    t = T_BLK
    nb = N_GROUP_BLOCKS
    pw, bbt, cs, a = jax.vmap(_s5_sources)(lam_re, lam_im, log_dt, b_re, b_im, c_re, c_im)
    coef_spec = pl.BlockSpec((1, N_DIR, 2, 1, LANES, SSM_STATE), lambda l, g: (l, 0, 0, g, 0, 0))
    shape = lambda r, c: jax.ShapeDtypeStruct((DEPTH, nb, r, c), BF16)
    spec = lambda r, c: pl.BlockSpec((1, 1, r, c), lambda l, g: (l, g, 0, 0))
    bc, cc, toep = pl.pallas_call(
        _s5_build_kernel,
        grid=(DEPTH, nb),
        in_specs=[
            pl.BlockSpec((1, N_DIR, 2, t, 2, 1, GROUPS_PER_BLOCK, SSM_STATE), lambda l, g: (l, 0, 0, 0, 0, g, 0, 0)),
            coef_spec,
            coef_spec,
        ],
        out_specs=[spec(t * LANES, N_DIR * STATE_LANES), spec(N_DIR * STATE_LANES, t * LANES),
                   spec(t * LANES, t * LANES)],
        out_shape=[shape(t * LANES, N_DIR * STATE_LANES), shape(N_DIR * STATE_LANES, t * LANES),
                   shape(t * LANES, t * LANES)],
        compiler_params=_params("parallel", "parallel"),
        name="s5_build",
    )(pw, bbt, cs)
    return bc, cc, toep, a


def _init_states(st):
    b = st.shape[0]
    s = st.reshape(b, N_DIR, 2, N_GROUP_BLOCKS, STATE_LANES // 2)
    return s.transpose(3, 1, 0, 2, 4).reshape(N_GROUP_BLOCKS, N_DIR, b, STATE_LANES)


def _final_states(fin):
    b = fin.shape[2]
    s = fin.reshape(N_GROUP_BLOCKS, N_DIR, b, 2, STATE_LANES // 2)
    return s.transpose(2, 1, 3, 0, 4).reshape(b, N_DIR, 2, SSM_GROUPS, SSM_STATE)


def kernel(x_prompt, x_sample, state_ssm, c, c_ctx, w_ada, b_ada, g_mix, w_in, ssm_lam_re, ssm_lam_im, ssm_log_dt, ssm_b_re, ssm_b_im, ssm_c_re, ssm_c_im, ssm_d, w_glu, conv_w, conv_b, w_out, g_ffn, w_gate, w_up, w_down, g_final):
    n_ctx, l_ctx, _ = x_prompt.shape
    n_dec, l_dec, _ = x_sample.shape

    cond = jnp.zeros((16, D_MODEL), F32).at[0].set(c_ctx).at[1:1 + n_dec].set(c)
    mod = _modulation(cond, w_ada, b_ada).reshape(DEPTH, 16, N_MOD, D_MODEL)
    mod_ctx = jnp.broadcast_to(mod[:, 0, :, None, :], (DEPTH, N_MOD, SUBLANES, D_MODEL))
    mod_dec = mod[:, 1:1 + n_dec].transpose(0, 2, 1, 3)

    w_in_t = _cast_col_tiles(w_in, TN_INPROJ, parts=4)
    w_glu_b = _cast_rows(w_glu, 512)
    w_out_b = _cast_rows(w_out, 512)
    w_gate_up_b = _cast_pair_col_tiles(w_gate, w_up, TF_FFN)
    w_down_b = _cast_rows(w_down, TF_FFN)
    g_mix3 = g_mix[:, None, :]
    g_ffn3 = g_ffn[:, None, :]
    conv_b3 = conv_b[:, None, :]
    ssm_d3 = ssm_d[:, None, :]

    groups = [
        dict(x=x_prompt, batch=n_ctx, seg=l_ctx, mod=mod_ctx, s0=None),
        dict(x=x_sample, batch=n_dec, seg=GRID_W, mod=mod_dec, s0=state_ssm.astype(F32)),
    ]

    bc, cc, toep, a = _s5_matrices(ssm_lam_re, ssm_lam_im, ssm_log_dt, ssm_b_re, ssm_b_im, ssm_c_re, ssm_c_im)
    finals = []
    for l in range(DEPTH):
        for grp in groups:
            batch = grp["batch"]
            s0 = None if grp["s0"] is None else _init_states(grp["s0"][:, l])
            if l == 0:
                x, u, gb, zc = _inproj(grp["x"], grp["mod"], g_mix3, w_in_t, l, batch_major=True)
            else:
                x = grp["x"]
                u, gb, zc = _inproj(x, grp["mod"], g_mix3, w_in_t, l, batch_major=False)
            y, fin = _s5(u, bc, cc, toep, a, ssm_d3, s0, batch, l)
            x1, h2 = _mixout(y, gb, zc, x, grp["mod"], w_glu_b, w_out_b, conv_w, conv_b3, g_ffn3,
                             batch, grp["seg"], l)
            grp["x"] = _ffn(h2, x1, grp["mod"], w_gate_up_b, w_down_b, g_final[None], l, batch,
                            final=(l == DEPTH - 1))
            if grp["s0"] is None:
                finals.append(_final_states(fin))

    new_state = jnp.stack(finals, axis=1).astype(x_prompt.dtype)
    return (groups[0]["x"], groups[1]["x"], new_state)
```

```python
import functools

import jax
import jax.numpy as jnp
from jax import lax
from jax.experimental import pallas as pl
from jax.experimental.pallas import tpu as pltpu

D_MODEL = 2048
DEPTH = 2
GRID_W = 64
D_SSM = 1024
D_CONV = 1024
SSM_H = 16
SSM_GROUPS = 64
SSM_STATE = 64
N_DIR = 2
D_FF = 5632
N_MOD = 6
EPS = 1e-6
LAM_RE_MAX = -1e-4

SUBLANES = 8
LANES = 128
GROUPS_PER_BLOCK = LANES // SSM_H
N_GROUP_BLOCKS = SSM_GROUPS // GROUPS_PER_BLOCK
PAIRS = GROUPS_PER_BLOCK // 2
STATE_LANES = GROUPS_PER_BLOCK * 2 * SSM_STATE
SCAN_ROWS = 1024
HALO_ROWS = 16
T_BLK = 4
VMEM_LIMIT = 58 * 1024 * 1024

TM_INPROJ = 1024
TM_MIXOUT = 512
TM_FFN = 512
TM_FFN_ACC = 1024
TN_INPROJ = 2048
TF_FFN = 512

F32 = jnp.float32
BF16 = jnp.bfloat16


def _params(*sem):
    return pltpu.CompilerParams(dimension_semantics=sem, vmem_limit_bytes=VMEM_LIMIT)


def _resident(shape, index_map):
    return pl.BlockSpec(shape, index_map, pipeline_mode=pl.Buffered(1))


def _rows8(a):
    return a.reshape(a.shape[0] // SUBLANES, SUBLANES, a.shape[1])


def _norm_mod(x, gain, scale, shift):
    ms = jnp.mean(x * x, axis=-1, keepdims=True)
    y = x * lax.rsqrt(ms + EPS) * gain
    h = _rows8(y) * (1.0 + scale)[None] + shift[None]
    return h.reshape(x.shape)


def _cast_kernel(w_ref, o_ref):
    o_ref[...] = w_ref[...].astype(BF16).reshape(o_ref.shape)


def _cast_col_tiles(w, tn, parts=1):
    _, k, n = w.shape
    nt = n // tn
    q = tn // parts
    return pl.pallas_call(
        _cast_kernel,
        grid=(DEPTH, nt, parts),
        in_specs=[pl.BlockSpec((1, k, q), lambda l, j, p: (l, 0, p * nt + j))],
        out_specs=pl.BlockSpec((1, 1, k, q), lambda l, j, p: (l, j, 0, p)),
        out_shape=jax.ShapeDtypeStruct((DEPTH, nt, k, tn), BF16),
        compiler_params=_params("parallel", "parallel", "parallel"),
        name="cast_col_tiles",
    )(w)


def _cast_pair_kernel(a_ref, b_ref, o_ref):
    half = a_ref.shape[2] // 2
    for s in range(2):
        o_ref[0, 0, :, (2 * s) * half:(2 * s + 1) * half] = a_ref[0, :, s * half:(s + 1) * half].astype(BF16)
        o_ref[0, 0, :, (2 * s + 1) * half:(2 * s + 2) * half] = b_ref[0, :, s * half:(s + 1) * half].astype(BF16)


def _cast_pair_col_tiles(wa, wb, tn):
    _, k, n = wa.shape
    spec = pl.BlockSpec((1, k, tn), lambda l, j: (l, 0, j))
    return pl.pallas_call(
        _cast_pair_kernel,
        grid=(DEPTH, n // tn),
        in_specs=[spec, spec],
        out_specs=pl.BlockSpec((1, 1, k, 2 * tn), lambda l, j: (l, j, 0, 0)),
        out_shape=jax.ShapeDtypeStruct((DEPTH, n // tn, k, 2 * tn), BF16),
        compiler_params=_params("parallel", "parallel"),
        name="cast_pair_col_tiles",
    )(wa, wb)


def _cast_rows(w, tk):
    _, k, n = w.shape
    return pl.pallas_call(
        _cast_kernel,
        grid=(DEPTH, k // tk),
        in_specs=[pl.BlockSpec((1, tk, n), lambda l, i: (l, i, 0))],
        out_specs=pl.BlockSpec((1, tk, n), lambda l, i: (l, i, 0)),
        out_shape=jax.ShapeDtypeStruct((DEPTH, k, n), BF16),
        compiler_params=_params("parallel", "parallel"),
        name="cast_rows",
    )(w)


def _mod_kernel(cond_ref, w_ref, b_ref, o_ref):
    s = jax.nn.silu(cond_ref[...]).astype(BF16)
    o_ref[0] = jnp.dot(s, w_ref[0].astype(BF16), preferred_element_type=F32) + b_ref[0]


def _modulation(cond, w_ada, b_ada):
    tn = 1024
    n = N_MOD * D_MODEL
    return pl.pallas_call(
        _mod_kernel,
        grid=(DEPTH, n // tn),
        in_specs=[
            pl.BlockSpec((16, D_MODEL), lambda l, j: (0, 0)),
            pl.BlockSpec((1, D_MODEL, tn), lambda l, j: (l, 0, j)),
            pl.BlockSpec((1, 1, tn), lambda l, j: (l, 0, j)),
        ],
        out_specs=pl.BlockSpec((1, 16, tn), lambda l, j: (l, 0, j)),
        out_shape=jax.ShapeDtypeStruct((DEPTH, 16, n), F32),
        compiler_params=_params("parallel", "parallel"),
        name="adaln_modulation",
    )(cond, w_ada, b_ada.reshape(DEPTH, 1, n))


def _inproj_kernel(x_ref, mod_ref, g_ref, w_ref, *out_refs, batch_major):
    if batch_major:
        xt_ref, u_ref, gb_ref, zc_ref, h_even, h_odd = out_refs
    else:
        u_ref, gb_ref, zc_ref, h_even, h_odd = out_refs
    i = pl.program_id(0)
    j = pl.program_id(1)
    tq = h_even.shape[0] // pl.num_programs(1)

    def prepare(h_next):
        if batch_major:
            x = jnp.concatenate([x_ref[:, t, :] for t in range(x_ref.shape[1])], axis=0)
            xt_ref[...] = x
        else:
            x = x_ref[...]
        h = _norm_mod(x, g_ref[0], mod_ref[0, 1], mod_ref[0, 0])
        h_next[pl.ds(pl.multiple_of(j * tq, tq), tq), :] = h.astype(BF16)

    def multiply(h_cur):
        z = jnp.dot(h_cur[...], w_ref[0, 0], preferred_element_type=F32)
        q = z.shape[1] // 4
        u_ref[...] = z[:, :q]
        gb_ref[...] = z[:, q:2 * q].astype(gb_ref.dtype)
        zc_ref[...] = (z[:, 2 * q:3 * q] * z[:, 3 * q:]).astype(zc_ref.dtype)

    n = pl.num_programs(0) - 1
    bufs = (h_even, h_odd)

    @pl.when(i == 0)
    def _():
        prepare(h_even)

    for parity in range(2):
        @pl.when((i > 0) & (i < n) & (i % 2 == parity))
        def _():
            prepare(bufs[parity])
            multiply(bufs[1 - parity])

        @pl.when((i == n) & (i % 2 == parity))
        def _():
            multiply(bufs[1 - parity])


def _inproj(x, mod, g_mix, w_in_tiled, layer, batch_major):
    tm, tn = TM_INPROJ, TN_INPROJ
    q = tn // 4
    nj = 4 * D_SSM // tn
    tq = tm // nj
    r = x.shape[0] * x.shape[1] if batch_major else x.shape[0]
    n = r // tm
    quarter = lambda i, j: jnp.minimum(i * nj + j, n * nj - 1)
    if batch_major:
        b = x.shape[0]
        x_spec = pl.BlockSpec((b, tq // b, D_MODEL), lambda i, j: (0, quarter(i, j), 0))
    else:
        x_spec = pl.BlockSpec((tq, D_MODEL), lambda i, j: (quarter(i, j), 0))
    out = jax.ShapeDtypeStruct((r, D_SSM), F32)
    out_specs = [pl.BlockSpec((tm, q), lambda i, j: (jnp.maximum(i - 1, 0), jnp.where(i == 0, 0, j)))] * 3
    gate = jax.ShapeDtypeStruct((r, D_CONV), BF16)
    out_shape = [out, gate, gate]
    if batch_major:
        out_specs = [pl.BlockSpec((tq, D_MODEL), lambda i, j: (quarter(i, j), 0))] + out_specs
        out_shape = [jax.ShapeDtypeStruct((r, D_MODEL), F32)] + out_shape
    return pl.pallas_call(
        functools.partial(_inproj_kernel, batch_major=batch_major),
        grid=(n + 1, nj),
        in_specs=[
            x_spec,
            pl.BlockSpec((1, N_MOD, SUBLANES, D_MODEL), lambda i, j: (layer, 0, 0, 0)),
            pl.BlockSpec((1, 1, D_MODEL), lambda i, j: (layer, 0, 0)),
            pl.BlockSpec((1, 1, D_MODEL, tn), lambda i, j: (layer, j, 0, 0)),
        ],
        out_specs=out_specs,
        out_shape=out_shape,
        scratch_shapes=[pltpu.VMEM((tm, D_MODEL), BF16), pltpu.VMEM((tm, D_MODEL), BF16)],
        compiler_params=_params("arbitrary", "arbitrary"),
        name="inproj",
    )(x, mod, g_mix, w_in_tiled)


def _s5_kernel(*refs, batch, has_init):
    if has_init:
        u_ref, bc_ref, cc_ref, toep_ref, a_ref, d_ref, s0_ref, y_ref, fin_ref, sf, sb, xs = refs
    else:
        u_ref, bc_ref, cc_ref, toep_ref, a_ref, d_ref, y_ref, fin_ref, sf, sb, xs = refs
        s0_ref = None
    rows = u_ref.shape[0]
    rc = SCAN_ROWS
    n_chunks = rows // rc
    nblk = rc // (batch * T_BLK)
    rblk = nblk * batch
    halves = batch // SUBLANES
    tl = T_BLK * LANES

    y_ref[...] = d_ref[0] * u_ref[...]
    if has_init:
        xs[...] = s0_ref[0]
    else:
        xs[...] = jnp.zeros_like(xs)

    def gather(r0):
        uc = u_ref[pl.ds(r0, rc), :].reshape(nblk, T_BLK, batch, LANES)
        return jnp.concatenate([uc[:, i].reshape(rblk, LANES) for i in range(T_BLK)], axis=1).astype(BF16)

    def scatter_add(r0, yb):
        parts = [yb[:, i * LANES:(i + 1) * LANES].reshape(nblk, 1, batch, LANES) for i in range(T_BLK)]
        y_ref[pl.ds(r0, rc), :] += jnp.concatenate(parts, axis=1).reshape(rc, LANES)

    def chunk(ci, carry):
        rf = pl.multiple_of(ci * rc, rc)
        rb = pl.multiple_of((n_chunks - 1 - ci) * rc, rc)
        lf = gather(rf)
        lb = gather(rb)
        sf[...] = jnp.dot(lf, bc_ref[0, 0, :, :STATE_LANES], preferred_element_type=F32)
        sb[...] = jnp.dot(lb, bc_ref[0, 0, :, STATE_LANES:], preferred_element_type=F32)
        for half in range(halves):
            hs = slice(half * SUBLANES, (half + 1) * SUBLANES)
            for pair in range(PAIRS):
                re = slice(pair * LANES, (pair + 1) * LANES)
                im = slice(STATE_LANES // 2 + pair * LANES, STATE_LANES // 2 + (pair + 1) * LANES)
                for d, s in ((0, sf), (1, sb)):
                    al = slice(d * PAIRS * LANES + pair * LANES, d * PAIRS * LANES + (pair + 1) * LANES)
                    ar = a_ref[0, 0, 0, :, al]
                    ai = a_ref[0, 0, 1, :, al]
                    xr = xs[d, hs, re]
                    xi = xs[d, hs, im]
                    for t in range(nblk):
                        blk = t if d == 0 else nblk - 1 - t
                        rs = slice(blk * batch + half * SUBLANES, blk * batch + (half + 1) * SUBLANES)
                        sr = s[rs, re]
                        si = s[rs, im]
                        s[rs, re] = xr
                        s[rs, im] = xi
                        xr, xi = ar * xr - ai * xi + sr, ar * xi + ai * xr + si
                    xs[d, hs, re] = xr
                    xs[d, hs, im] = xi
        yf = jnp.dot(sf[...].astype(BF16), cc_ref[0, 0, :STATE_LANES, :], preferred_element_type=F32)
        yf += jnp.dot(lf, toep_ref[0, 0], preferred_element_type=F32)
        yb = jnp.dot(sb[...].astype(BF16), cc_ref[0, 0, STATE_LANES:, :], preferred_element_type=F32)
        scatter_add(rf, yf)
        scatter_add(rb, yb)
        return carry

    lax.fori_loop(0, n_chunks, chunk, 0)
    fin_ref[0] = xs[...]


def _s5(u, bc, cc, toep, a, d_skip, s0, batch, layer):
    r = u.shape[0]
    rblk = SCAN_ROWS // T_BLK
    tl = T_BLK * LANES
    has_init = s0 is not None
    in_specs = [
        pl.BlockSpec((r, LANES), lambda g: (0, g)),
        pl.BlockSpec((1, 1, tl, N_DIR * STATE_LANES), lambda g: (layer, g, 0, 0)),
        pl.BlockSpec((1, 1, N_DIR * STATE_LANES, tl), lambda g: (layer, g, 0, 0)),
        pl.BlockSpec((1, 1, tl, tl), lambda g: (layer, g, 0, 0)),
        pl.BlockSpec((1, 1, 2, SUBLANES, N_DIR * PAIRS * LANES), lambda g: (layer, g, 0, 0, 0)),
        pl.BlockSpec((1, 1, LANES), lambda g: (layer, 0, g)),
    ]
    args = [u, bc, cc, toep, a, d_skip]
    if has_init:
        in_specs.append(pl.BlockSpec((1, N_DIR, batch, STATE_LANES), lambda g: (g, 0, 0, 0)))
        args.append(s0)
    return pl.pallas_call(
        functools.partial(_s5_kernel, batch=batch, has_init=has_init),
        grid=(N_GROUP_BLOCKS,),
        in_specs=in_specs,
        out_specs=[
            pl.BlockSpec((r, LANES), lambda g: (0, g)),
            pl.BlockSpec((1, N_DIR, batch, STATE_LANES), lambda g: (g, 0, 0, 0)),
        ],
        out_shape=[
            jax.ShapeDtypeStruct((r, D_SSM), F32),
            jax.ShapeDtypeStruct((N_GROUP_BLOCKS, N_DIR, batch, STATE_LANES), F32),
        ],
        scratch_shapes=[
            pltpu.VMEM((rblk, STATE_LANES), F32),
            pltpu.VMEM((rblk, STATE_LANES), F32),
            pltpu.VMEM((N_DIR, batch, STATE_LANES), F32),
        ],
        compiler_params=_params("parallel"),
        name="s5_mixer",
    )(*args)


def _mixout_kernel(y_ref, gb_ref, zc_ref, zp_ref, zn_ref, x_ref, mod_ref, wglu_ref, wout_ref,
                   cw_ref, cb_ref, g_ref, x1_ref, h2_ref, *, batch, seg):
    tm = x_ref.shape[0]
    tl = tm // batch
    l0 = pl.program_id(0) * tl
    prev_ok = (l0 % seg != 0).astype(F32)
    next_ok = ((l0 + tl) % seg != 0).astype(F32)

    y = jax.nn.gelu(y_ref[...])
    y_ssm = y * jax.nn.sigmoid(jnp.dot(y.astype(BF16), wglu_ref[0], preferred_element_type=F32))

    zc = zc_ref[...].astype(F32)
    zp = zp_ref[...].astype(F32)[HALO_ROWS - batch:, :]
    zn = zn_ref[...].astype(F32)[:batch, :]
    prev = jnp.concatenate([zp * prev_ok, zc[:tm - batch]], axis=0)
    nxt = jnp.concatenate([zc[batch:], zn * next_ok], axis=0)
    conv = prev * cw_ref[0, 0:1, :] + zc * cw_ref[0, 1:2, :] + nxt * cw_ref[0, 2:3, :] + cb_ref[0]
    y_conv = gb_ref[...].astype(F32) * conv

    out = jnp.dot(y_conv.astype(BF16), wout_ref[0, D_SSM:, :], preferred_element_type=F32)
    out += jnp.dot(y_ssm.astype(BF16), wout_ref[0, :D_SSM, :], preferred_element_type=F32)
    x1 = (_rows8(x_ref[...]) + mod_ref[0, 2][None] * _rows8(out)).reshape(tm, D_MODEL)
    x1_ref[...] = x1
    h2_ref[...] = _norm_mod(x1, g_ref[0], mod_ref[0, 4], mod_ref[0, 3]).astype(BF16)


def _mixout(y, gb, zc, x, mod, w_glu, w_out, conv_w, conv_b, g_ffn, batch, seg, layer):
    r = x.shape[0]
    tm = TM_MIXOUT
    assert seg % (tm // batch) == 0
    hb = tm // HALO_ROWS
    n_hb = r // HALO_ROWS
    row = lambda i: (i, 0)
    lay = lambda i: (layer, 0, 0)
    return pl.pallas_call(
        functools.partial(_mixout_kernel, batch=batch, seg=seg),
        grid=(r // tm,),
        in_specs=[
            pl.BlockSpec((tm, D_SSM), row),
            pl.BlockSpec((tm, D_CONV), row),
            pl.BlockSpec((tm, D_CONV), row),
            pl.BlockSpec((HALO_ROWS, D_CONV), lambda i: (jnp.maximum(i * hb - 1, 0), 0)),
            pl.BlockSpec((HALO_ROWS, D_CONV), lambda i: (jnp.minimum((i + 1) * hb, n_hb - 1), 0)),
            pl.BlockSpec((tm, D_MODEL), row),
            _resident((1, N_MOD, SUBLANES, D_MODEL), lambda i: (layer, 0, 0, 0)),
            _resident((1, D_SSM, D_SSM), lay),
            _resident((1, D_MODEL, D_MODEL), lay),
            _resident((1, 3, D_CONV), lay),
            _resident((1, 1, D_CONV), lay),
            _resident((1, 1, D_MODEL), lay),
        ],
        out_specs=[pl.BlockSpec((tm, D_MODEL), row), pl.BlockSpec((tm, D_MODEL), row)],
        out_shape=[jax.ShapeDtypeStruct((r, D_MODEL), F32), jax.ShapeDtypeStruct((r, D_MODEL), BF16)],
        compiler_params=_params("parallel"),
        name="mixer_out",
    )(y, gb, zc, zc, zc, x, mod, w_glu, w_out, conv_w, conv_b, g_ffn)


def _ffn_kernel(h_ref, x_ref, mod_ref, wgu_ref, wd_ref, gf_ref, o_ref, *scratch, final):
    k = pl.program_id(1)
    acc = scratch[0] if final else o_ref

    @pl.when(k == 0)
    def _():
        acc[...] = jnp.zeros_like(acc)

    gate_up = jnp.dot(h_ref[...], wgu_ref[0, 0], preferred_element_type=F32)
    hw = gate_up.shape[1] // 4
    act = jnp.concatenate(
        [jax.nn.silu(gate_up[:, 2 * s * hw:(2 * s + 1) * hw]) * gate_up[:, (2 * s + 1) * hw:(2 * s + 2) * hw]
         for s in range(2)], axis=1).astype(BF16)
    acc[...] += jnp.dot(act, wd_ref[0], preferred_element_type=F32)

    @pl.when(k == pl.num_programs(1) - 1)
    def _():
        x2 = (_rows8(x_ref[...]) + mod_ref[0, 5][None] * _rows8(acc[...])).reshape(x_ref.shape)
        if final:
            ms = jnp.mean(x2 * x2, axis=-1, keepdims=True)
            x2 = x2 * lax.rsqrt(ms + EPS) * gf_ref[...]
            b = o_ref.shape[0]
            for t in range(o_ref.shape[1]):
                o_ref[:, t, :] = x2[t * b:(t + 1) * b, :]
        else:
            o_ref[...] = x2


def _ffn(h2, x1, mod, w_gate_up, w_down, g_final, layer, batch, final):
    r = x1.shape[0]
    tf = TF_FFN
    if final:
        tm = TM_FFN
        out_spec = pl.BlockSpec((batch, tm // batch, D_MODEL), lambda i, k: (0, i, 0))
        out_shape = jax.ShapeDtypeStruct((batch, r // batch, D_MODEL), F32)
        scratch = [pltpu.VMEM((tm, D_MODEL), F32)]
    else:
        tm = TM_FFN_ACC
        out_spec = pl.BlockSpec((tm, D_MODEL), lambda i, k: (i, 0))
        out_shape = jax.ShapeDtypeStruct((r, D_MODEL), F32)
        scratch = []
    return pl.pallas_call(
        functools.partial(_ffn_kernel, final=final),
        grid=(r // tm, D_FF // tf),
        in_specs=[
            pl.BlockSpec((tm, D_MODEL), lambda i, k: (i, 0)),
            pl.BlockSpec((tm, D_MODEL), lambda i, k: (i, 0)),
            pl.BlockSpec((1, N_MOD, SUBLANES, D_MODEL), lambda i, k: (layer, 0, 0, 0)),
            pl.BlockSpec((1, 1, D_MODEL, 2 * tf), lambda i, k: (layer, k, 0, 0)),
            pl.BlockSpec((1, tf, D_MODEL), lambda i, k: (layer, k, 0)),
            pl.BlockSpec((1, D_MODEL), lambda i, k: (0, 0)),
        ],
        out_specs=out_spec,
        out_shape=out_shape,
        scratch_shapes=scratch,
        compiler_params=_params("parallel", "arbitrary"),
        name="ffn",
    )(h2, x1, mod, w_gate_up, w_down, g_final)


def _s5_build_kernel(pw_ref, bb_ref, cs_ref, bc_ref, cc_ref, toep_ref):
    row_group = lax.broadcasted_iota(jnp.int32, (LANES, LANES), 0) // SSM_H
    lane = lax.broadcasted_iota(jnp.int32, (LANES, LANES), 1)
    row_group_s = lax.broadcasted_iota(jnp.int32, (LANES, SSM_STATE), 0) // SSM_H

    def per_row(p):
        out = jnp.zeros((LANES, SSM_STATE), F32)
        for g in range(GROUPS_PER_BLOCK):
            out = jnp.where(row_group_s == g, p[g:g + 1, :], out)
        return out

    dup_state = (lax.broadcasted_iota(jnp.int32, (SSM_STATE, LANES), 1) % SSM_STATE
                 == lax.broadcasted_iota(jnp.int32, (SSM_STATE, LANES), 0)).astype(BF16)
    keep = [(row_group // 2 == q) & (lane // SSM_STATE == row_group % 2) for q in range(PAIRS)]
    over_states = (((1,), (1,)), ((), ()))
    taps = {}
    for d in range(N_DIR):
        b_re, b_im = bb_ref[0, d, 0, 0], bb_ref[0, d, 1, 0]
        c_re, c_im = cs_ref[0, d, 0, 0], cs_ref[0, d, 1, 0]
        for i in range(T_BLK):
            pr, pi = per_row(pw_ref[0, d, 0, i, 0, 0]), per_row(pw_ref[0, d, 0, i, 1, 0])
            qr, qi = per_row(pw_ref[0, d, 1, i, 0, 0]), per_row(pw_ref[0, d, 1, i, 1, 0])
            src_b = (b_re * pr - b_im * pi, b_re * pi + b_im * pr)
            src_c = (c_re * qr - c_im * qi, -(c_re * qi + c_im * qr))
            lag = T_BLK - 1 - i if d == 0 else i
            taps[d, lag] = (
                lax.dot_general(src_b[0], c_re, over_states, precision=lax.Precision.HIGHEST,
                                preferred_element_type=F32)
                - lax.dot_general(src_b[1], c_im, over_states, precision=lax.Precision.HIGHEST,
                                  preferred_element_type=F32))
            for r in range(2):
                wide_b = jnp.dot(src_b[r].astype(BF16), dup_state, preferred_element_type=F32)
                wide_c = jnp.dot(src_c[r].astype(BF16), dup_state, preferred_element_type=F32)
                for q in range(PAIRS):
                    c0 = d * STATE_LANES + r * (STATE_LANES // 2) + q * LANES
                    bc_ref[0, 0, i * LANES:(i + 1) * LANES, c0:c0 + LANES] = (
                        jnp.where(keep[q], wide_b, 0.0).astype(BF16))
                    cc_ref[0, 0, c0:c0 + LANES, i * LANES:(i + 1) * LANES] = (
                        jnp.where(keep[q], wide_c, 0.0).T.astype(BF16))
    diag = row_group == lane // SSM_H
    for i_in in range(T_BLK):
        for i_out in range(T_BLK):
            lag = i_out - i_in
            tap = taps[0, lag] if lag > 0 else taps[1, -lag] if lag < 0 else taps[0, 0] + taps[1, 0]
            toep_ref[0, 0, i_in * LANES:(i_in + 1) * LANES, i_out * LANES:(i_out + 1) * LANES] = (
                jnp.where(diag, tap, 0.0).astype(BF16))


def _cpow(lr, li, dt, k):
    m = jnp.exp(lr * dt * k)
    return m * jnp.cos(li * dt * k), m * jnp.sin(li * dt * k)


def _s5_sources(lam_re, lam_im, log_dt, b_re, b_im, c_re, c_im):
    t = T_BLK
    nb = N_GROUP_BLOCKS
    lr = jnp.minimum(lam_re, LAM_RE_MAX)
    li = lam_im
    dt = jnp.exp(log_dt)[..., None]
    ar, ai = _cpow(lr, li, dt, 1.0)
    den = lr * lr + li * li
    qr = (((ar - 1.0) * lr + ai * li) / den)[:, :, None, :]
    qi = ((ai * lr - (ar - 1.0) * li) / den)[:, :, None, :]
    b_re_t = b_re.transpose(0, 1, 3, 2)
    b_im_t = b_im.transpose(0, 1, 3, 2)
    bbr = qr * b_re_t - qi * b_im_t
    bbi = qr * b_im_t + qi * b_re_t

    steps = jnp.arange(t, dtype=F32)
    e_in = jnp.stack([t - 1.0 - steps, steps], axis=0)[:, :, None, None]
    e_out = jnp.stack([steps + 1.0, t - steps], axis=0)[:, :, None, None]
    lr4, li4, dt4 = lr[:, None], li[:, None], dt[:, None]
    pw = jnp.stack([jnp.stack(_cpow(lr4, li4, dt4, e_in), axis=2),
                    jnp.stack(_cpow(lr4, li4, dt4, e_out), axis=2)], axis=1)
    pw = pw.reshape(N_DIR, 2, t, 2, nb, GROUPS_PER_BLOCK, SSM_STATE)
    bbt = jnp.stack([bbr, bbi], axis=1).reshape(N_DIR, 2, nb, LANES, SSM_STATE)
    cs = jnp.stack([c_re, c_im], axis=1).reshape(N_DIR, 2, nb, LANES, SSM_STATE)

    atr, ati = _cpow(lr, li, dt, float(t))
    a = jnp.stack([atr, ati], axis=0)
    a = a.reshape(2, N_DIR, nb, PAIRS * LANES).transpose(2, 0, 1, 3).reshape(nb, 2, 1, N_DIR * PAIRS * LANES)
    a = jnp.broadcast_to(a, (nb, 2, SUBLANES, N_DIR * PAIRS * LANES))
    return pw, bbt, cs, a


def _s5_matrices(lam_re, lam_im, log_dt, b_re, b_im, c_re, c_im):
    t = T_BLK
    nb = N_GROUP_BLOCKS
    pw, bbt, cs, a = jax.vmap(_s5_sources)(lam_re, lam_im, log_dt, b_re, b_im, c_re, c_im)
    coef_spec = pl.BlockSpec((1, N_DIR, 2, 1, LANES, SSM_STATE), lambda l, g: (l, 0, 0, g, 0, 0))
    shape = lambda r, c: jax.ShapeDtypeStruct((DEPTH, nb, r, c), BF16)
    spec = lambda r, c: pl.BlockSpec((1, 1, r, c), lambda l, g: (l, g, 0, 0))
    bc, cc, toep = pl.pallas_call(
        _s5_build_kernel,
        grid=(DEPTH, nb),
        in_specs=[
            pl.BlockSpec((1, N_DIR, 2, t, 2, 1, GROUPS_PER_BLOCK, SSM_STATE), lambda l, g: (l, 0, 0, 0, 0, g, 0, 0)),
            coef_spec,
            coef_spec,
        ],
        out_specs=[spec(t * LANES, N_DIR * STATE_LANES), spec(N_DIR * STATE_LANES, t * LANES),
                   spec(t * LANES, t * LANES)],
        out_shape=[shape(t * LANES, N_DIR * STATE_LANES), shape(N_DIR * STATE_LANES, t * LANES),
                   shape(t * LANES, t * LANES)],
        compiler_params=_params("parallel", "parallel"),
        name="s5_build",
    )(pw, bbt, cs)
    return bc, cc, toep, a


def _init_states(st):
    b = st.shape[0]
    s = st.reshape(b, N_DIR, 2, N_GROUP_BLOCKS, STATE_LANES // 2)
    return s.transpose(3, 1, 0, 2, 4).reshape(N_GROUP_BLOCKS, N_DIR, b, STATE_LANES)


def _final_states(fin):
    b = fin.shape[2]
    s = fin.reshape(N_GROUP_BLOCKS, N_DIR, b, 2, STATE_LANES // 2)
    return s.transpose(2, 1, 3, 0, 4).reshape(b, N_DIR, 2, SSM_GROUPS, SSM_STATE)


def kernel(x_prompt, x_sample, state_ssm, c, c_ctx, w_ada, b_ada, g_mix, w_in, ssm_lam_re, ssm_lam_im, ssm_log_dt, ssm_b_re, ssm_b_im, ssm_c_re, ssm_c_im, ssm_d, w_glu, conv_w, conv_b, w_out, g_ffn, w_gate, w_up, w_down, g_final):
    n_ctx, l_ctx, _ = x_prompt.shape
    n_dec, l_dec, _ = x_sample.shape

    cond = jnp.zeros((16, D_MODEL), F32).at[0].set(c_ctx).at[1:1 + n_dec].set(c)
    mod = _modulation(cond, w_ada, b_ada).reshape(DEPTH, 16, N_MOD, D_MODEL)
    mod_ctx = jnp.broadcast_to(mod[:, 0, :, None, :], (DEPTH, N_MOD, SUBLANES, D_MODEL))
    mod_dec = mod[:, 1:1 + n_dec].transpose(0, 2, 1, 3)

    w_in_t = _cast_col_tiles(w_in, TN_INPROJ, parts=4)
    w_glu_b = _cast_rows(w_glu, 512)
    w_out_b = _cast_rows(w_out, 512)
    w_gate_up_b = _cast_pair_col_tiles(w_gate, w_up, TF_FFN)
    w_down_b = _cast_rows(w_down, TF_FFN)
    g_mix3 = g_mix[:, None, :]
    g_ffn3 = g_ffn[:, None, :]
    conv_b3 = conv_b[:, None, :]
    ssm_d3 = ssm_d[:, None, :]

    groups = [
        dict(x=x_prompt, batch=n_ctx, seg=l_ctx, mod=mod_ctx, s0=None),
        dict(x=x_sample, batch=n_dec, seg=GRID_W, mod=mod_dec, s0=state_ssm.astype(F32)),
    ]

    bc, cc, toep, a = _s5_matrices(ssm_lam_re, ssm_lam_im, ssm_log_dt, ssm_b_re, ssm_b_im, ssm_c_re, ssm_c_im)
    finals = []
    for l in range(DEPTH):
        for grp in groups:
            batch = grp["batch"]
            s0 = None if grp["s0"] is None else _init_states(grp["s0"][:, l])
            if l == 0:
                x, u, gb, zc = _inproj(grp["x"], grp["mod"], g_mix3, w_in_t, l, batch_major=True)
            else:
                x = grp["x"]
                u, gb, zc = _inproj(x, grp["mod"], g_mix3, w_in_t, l, batch_major=False)
            y, fin = _s5(u, bc, cc, toep, a, ssm_d3, s0, batch, l)
            x1, h2 = _mixout(y, gb, zc, x, grp["mod"], w_glu_b, w_out_b, conv_w, conv_b3, g_ffn3,
                             batch, grp["seg"], l)
            grp["x"] = _ffn(h2, x1, grp["mod"], w_gate_up_b, w_down_b, g_final[None], l, batch,
                            final=(l == DEPTH - 1))
            if grp["s0"] is None:
                finals.append(_final_states(fin))

    new_state = jnp.stack(finals, axis=1).astype(x_prompt.dtype)
    return (groups[0]["x"], groups[1]["x"], new_state)
```

```python
import functools

import jax
import jax.numpy as jnp
from jax import lax
from jax.experimental import pallas as pl
from jax.experimental.pallas import tpu as pltpu

D_MODEL = 2048
DEPTH = 2
GRID_W = 64
D_SSM = 1024
D_CONV = 1024
SSM_H = 16
SSM_GROUPS = 64
SSM_STATE = 64
N_DIR = 2
D_FF = 5632
N_MOD = 6
EPS = 1e-6
LAM_RE_MAX = -1e-4

SUBLANES = 8
LANES = 128
GROUPS_PER_BLOCK = LANES // SSM_H
N_GROUP_BLOCKS = SSM_GROUPS // GROUPS_PER_BLOCK
PAIRS = GROUPS_PER_BLOCK // 2
STATE_LANES = GROUPS_PER_BLOCK * 2 * SSM_STATE
SCAN_ROWS = 1024
HALO_ROWS = 16
T_BLK = 4
VMEM_LIMIT = 58 * 1024 * 1024

TM_INPROJ = 1024
TM_MIXOUT = 512
TM_FFN = 512
TM_FFN_ACC = 1024
TN_INPROJ = 1024
TF_FFN = 512

F32 = jnp.float32
BF16 = jnp.bfloat16


def _params(*sem):
    return pltpu.CompilerParams(dimension_semantics=sem, vmem_limit_bytes=VMEM_LIMIT)


def _resident(shape, index_map):
    return pl.BlockSpec(shape, index_map, pipeline_mode=pl.Buffered(1))


def _rows8(a):
    return a.reshape(a.shape[0] // SUBLANES, SUBLANES, a.shape[1])


def _norm_mod(x, gain, scale, shift):
    ms = jnp.mean(x * x, axis=-1, keepdims=True)
    y = x * lax.rsqrt(ms + EPS) * gain
    h = _rows8(y) * (1.0 + scale)[None] + shift[None]
    return h.reshape(x.shape)


def _cast_kernel(w_ref, o_ref):
    o_ref[...] = w_ref[...].astype(BF16).reshape(o_ref.shape)


def _cast_col_tiles(w, tn, parts=1):
    _, k, n = w.shape
    nt = n // tn
    q = tn // parts
    return pl.pallas_call(
        _cast_kernel,
        grid=(DEPTH, nt, parts),
        in_specs=[pl.BlockSpec((1, k, q), lambda l, j, p: (l, 0, p * nt + j))],
        out_specs=pl.BlockSpec((1, 1, k, q), lambda l, j, p: (l, j, 0, p)),
        out_shape=jax.ShapeDtypeStruct((DEPTH, nt, k, tn), BF16),
        compiler_params=_params("parallel", "parallel", "parallel"),
        name="cast_col_tiles",
    )(w)


def _cast_pair_kernel(a_ref, b_ref, o_ref):
    half = a_ref.shape[2] // 2
    for s in range(2):
        o_ref[0, 0, :, (2 * s) * half:(2 * s + 1) * half] = a_ref[0, :, s * half:(s + 1) * half].astype(BF16)
        o_ref[0, 0, :, (2 * s + 1) * half:(2 * s + 2) * half] = b_ref[0, :, s * half:(s + 1) * half].astype(BF16)


def _cast_pair_col_tiles(wa, wb, tn):
    _, k, n = wa.shape
    spec = pl.BlockSpec((1, k, tn), lambda l, j: (l, 0, j))
    return pl.pallas_call(
        _cast_pair_kernel,
        grid=(DEPTH, n // tn),
        in_specs=[spec, spec],
        out_specs=pl.BlockSpec((1, 1, k, 2 * tn), lambda l, j: (l, j, 0, 0)),
        out_shape=jax.ShapeDtypeStruct((DEPTH, n // tn, k, 2 * tn), BF16),
        compiler_params=_params("parallel", "parallel"),
        name="cast_pair_col_tiles",
    )(wa, wb)


def _cast_rows(w, tk):
    _, k, n = w.shape
    return pl.pallas_call(
        _cast_kernel,
        grid=(DEPTH, k // tk),
        in_specs=[pl.BlockSpec((1, tk, n), lambda l, i: (l, i, 0))],
        out_specs=pl.BlockSpec((1, tk, n), lambda l, i: (l, i, 0)),
        out_shape=jax.ShapeDtypeStruct((DEPTH, k, n), BF16),
        compiler_params=_params("parallel", "parallel"),
        name="cast_rows",
    )(w)


def _mod_kernel(cond_ref, w_ref, b_ref, o_ref):
    s = jax.nn.silu(cond_ref[...]).astype(BF16)
    o_ref[0] = jnp.dot(s, w_ref[0].astype(BF16), preferred_element_type=F32) + b_ref[0]


def _modulation(cond, w_ada, b_ada):
    tn = 1024
    n = N_MOD * D_MODEL
    return pl.pallas_call(
        _mod_kernel,
        grid=(DEPTH, n // tn),
        in_specs=[
            pl.BlockSpec((16, D_MODEL), lambda l, j: (0, 0)),
            pl.BlockSpec((1, D_MODEL, tn), lambda l, j: (l, 0, j)),
            pl.BlockSpec((1, 1, tn), lambda l, j: (l, 0, j)),
        ],
        out_specs=pl.BlockSpec((1, 16, tn), lambda l, j: (l, 0, j)),
        out_shape=jax.ShapeDtypeStruct((DEPTH, 16, n), F32),
        compiler_params=_params("parallel", "parallel"),
        name="adaln_modulation",
    )(cond, w_ada, b_ada.reshape(DEPTH, 1, n))


def _inproj_kernel(x_ref, mod_ref, g_ref, w_ref, *out_refs, batch_major):
    if batch_major:
        xt_ref, u_ref, gb_ref, zc_ref, h_even, h_odd = out_refs
    else:
        u_ref, gb_ref, zc_ref, h_even, h_odd = out_refs
    i = pl.program_id(0)
    j = pl.program_id(1)
    tq = h_even.shape[0] // pl.num_programs(1)

    def prepare(h_next):
        if batch_major:
            x = jnp.concatenate([x_ref[:, t, :] for t in range(x_ref.shape[1])], axis=0)
            xt_ref[...] = x
        else:
            x = x_ref[...]
        h = _norm_mod(x, g_ref[0], mod_ref[0, 1], mod_ref[0, 0])
        h_next[pl.ds(pl.multiple_of(j * tq, tq), tq), :] = h.astype(BF16)

    def multiply(h_cur):
        z = jnp.dot(h_cur[...], w_ref[0, 0], preferred_element_type=F32)
        q = z.shape[1] // 4
        u_ref[...] = z[:, :q]
        gb_ref[...] = z[:, q:2 * q].astype(gb_ref.dtype)
        zc_ref[...] = (z[:, 2 * q:3 * q] * z[:, 3 * q:]).astype(zc_ref.dtype)

    n = pl.num_programs(0) - 1
    bufs = (h_even, h_odd)

    @pl.when(i == 0)
    def _():
        prepare(h_even)

    for parity in range(2):
        @pl.when((i > 0) & (i < n) & (i % 2 == parity))
        def _():
            prepare(bufs[parity])
            multiply(bufs[1 - parity])

        @pl.when((i == n) & (i % 2 == parity))
        def _():
            multiply(bufs[1 - parity])


def _inproj(x, mod, g_mix, w_in_tiled, layer, batch_major):
    tm, tn = TM_INPROJ, TN_INPROJ
    q = tn // 4
    nj = 4 * D_SSM // tn
    tq = tm // nj
    r = x.shape[0] * x.shape[1] if batch_major else x.shape[0]
    n = r // tm
    quarter = lambda i, j: jnp.minimum(i * nj + j, n * nj - 1)
    if batch_major:
        b = x.shape[0]
        x_spec = pl.BlockSpec((b, tq // b, D_MODEL), lambda i, j: (0, quarter(i, j), 0))
    else:
        x_spec = pl.BlockSpec((tq, D_MODEL), lambda i, j: (quarter(i, j), 0))
    out = jax.ShapeDtypeStruct((r, D_SSM), F32)
    out_specs = [pl.BlockSpec((tm, q), lambda i, j: (jnp.maximum(i - 1, 0), jnp.where(i == 0, 0, j)))] * 3
    gate = jax.ShapeDtypeStruct((r, D_CONV), BF16)
    out_shape = [out, gate, gate]
    if batch_major:
        out_specs = [pl.BlockSpec((tq, D_MODEL), lambda i, j: (quarter(i, j), 0))] + out_specs
        out_shape = [jax.ShapeDtypeStruct((r, D_MODEL), F32)] + out_shape
    return pl.pallas_call(
        functools.partial(_inproj_kernel, batch_major=batch_major),
        grid=(n + 1, nj),
        in_specs=[
            x_spec,
            pl.BlockSpec((1, N_MOD, SUBLANES, D_MODEL), lambda i, j: (layer, 0, 0, 0)),
            pl.BlockSpec((1, 1, D_MODEL), lambda i, j: (layer, 0, 0)),
            pl.BlockSpec((1, 1, D_MODEL, tn), lambda i, j: (layer, j, 0, 0)),
        ],
        out_specs=out_specs,
        out_shape=out_shape,
        scratch_shapes=[pltpu.VMEM((tm, D_MODEL), BF16), pltpu.VMEM((tm, D_MODEL), BF16)],
        compiler_params=_params("arbitrary", "arbitrary"),
        name="inproj",
    )(x, mod, g_mix, w_in_tiled)


def _s5_kernel(*refs, batch, has_init):
    if has_init:
        u_ref, bc_ref, cc_ref, toep_ref, a_ref, d_ref, s0_ref, y_ref, fin_ref, sf, sb, xs = refs
    else:
        u_ref, bc_ref, cc_ref, toep_ref, a_ref, d_ref, y_ref, fin_ref, sf, sb, xs = refs
        s0_ref = None
    rows = u_ref.shape[0]
    rc = SCAN_ROWS
    n_chunks = rows // rc
    nblk = rc // (batch * T_BLK)
    rblk = nblk * batch
    halves = batch // SUBLANES
    tl = T_BLK * LANES

    y_ref[...] = d_ref[0] * u_ref[...]
    if has_init:
        xs[...] = s0_ref[0]
    else:
        xs[...] = jnp.zeros_like(xs)

    def gather(r0):
        uc = u_ref[pl.ds(r0, rc), :].reshape(nblk, T_BLK, batch, LANES)
        return jnp.concatenate([uc[:, i].reshape(rblk, LANES) for i in range(T_BLK)], axis=1).astype(BF16)

    def scatter_add(r0, yb):
        parts = [yb[:, i * LANES:(i + 1) * LANES].reshape(nblk, 1, batch, LANES) for i in range(T_BLK)]
        y_ref[pl.ds(r0, rc), :] += jnp.concatenate(parts, axis=1).reshape(rc, LANES)

    def chunk(ci, carry):
        rf = pl.multiple_of(ci * rc, rc)
        rb = pl.multiple_of((n_chunks - 1 - ci) * rc, rc)
        lf = gather(rf)
        lb = gather(rb)
        sf[...] = jnp.dot(lf, bc_ref[0, 0, :, :STATE_LANES], preferred_element_type=F32)
        sb[...] = jnp.dot(lb, bc_ref[0, 0, :, STATE_LANES:], preferred_element_type=F32)
        for half in range(halves):
            hs = slice(half * SUBLANES, (half + 1) * SUBLANES)
            for pair in range(PAIRS):
                re = slice(pair * LANES, (pair + 1) * LANES)
                im = slice(STATE_LANES // 2 + pair * LANES, STATE_LANES // 2 + (pair + 1) * LANES)
                for d, s in ((0, sf), (1, sb)):
                    al = slice(d * PAIRS * LANES + pair * LANES, d * PAIRS * LANES + (pair + 1) * LANES)
                    ar = a_ref[0, 0, 0, :, al]
                    ai = a_ref[0, 0, 1, :, al]
                    xr = xs[d, hs, re]
                    xi = xs[d, hs, im]
                    for t in range(nblk):
                        blk = t if d == 0 else nblk - 1 - t
                        rs = slice(blk * batch + half * SUBLANES, blk * batch + (half + 1) * SUBLANES)
                        sr = s[rs, re]
                        si = s[rs, im]
                        s[rs, re] = xr
                        s[rs, im] = xi
                        xr, xi = ar * xr - ai * xi + sr, ar * xi + ai * xr + si
                    xs[d, hs, re] = xr
                    xs[d, hs, im] = xi
        yf = jnp.dot(sf[...].astype(BF16), cc_ref[0, 0, :STATE_LANES, :], preferred_element_type=F32)
        yf += jnp.dot(lf, toep_ref[0, 0], preferred_element_type=F32)
        yb = jnp.dot(sb[...].astype(BF16), cc_ref[0, 0, STATE_LANES:, :], preferred_element_type=F32)
        scatter_add(rf, yf)
        scatter_add(rb, yb)
        return carry

    lax.fori_loop(0, n_chunks, chunk, 0)
    fin_ref[0] = xs[...]


def _s5(u, bc, cc, toep, a, d_skip, s0, batch, layer):
    r = u.shape[0]
    rblk = SCAN_ROWS // T_BLK
    tl = T_BLK * LANES
    has_init = s0 is not None
    in_specs = [
        pl.BlockSpec((r, LANES), lambda g: (0, g)),
        pl.BlockSpec((1, 1, tl, N_DIR * STATE_LANES), lambda g: (layer, g, 0, 0)),
        pl.BlockSpec((1, 1, N_DIR * STATE_LANES, tl), lambda g: (layer, g, 0, 0)),
        pl.BlockSpec((1, 1, tl, tl), lambda g: (layer, g, 0, 0)),
        pl.BlockSpec((1, 1, 2, SUBLANES, N_DIR * PAIRS * LANES), lambda g: (layer, g, 0, 0, 0)),
        pl.BlockSpec((1, 1, LANES), lambda g: (layer, 0, g)),
    ]
    args = [u, bc, cc, toep, a, d_skip]
    if has_init:
        in_specs.append(pl.BlockSpec((1, N_DIR, batch, STATE_LANES), lambda g: (g, 0, 0, 0)))
        args.append(s0)
    return pl.pallas_call(
        functools.partial(_s5_kernel, batch=batch, has_init=has_init),
        grid=(N_GROUP_BLOCKS,),
        in_specs=in_specs,
        out_specs=[
            pl.BlockSpec((r, LANES), lambda g: (0, g)),
            pl.BlockSpec((1, N_DIR, batch, STATE_LANES), lambda g: (g, 0, 0, 0)),
        ],
        out_shape=[
            jax.ShapeDtypeStruct((r, D_SSM), F32),
            jax.ShapeDtypeStruct((N_GROUP_BLOCKS, N_DIR, batch, STATE_LANES), F32),
        ],
        scratch_shapes=[
            pltpu.VMEM((rblk, STATE_LANES), F32),
            pltpu.VMEM((rblk, STATE_LANES), F32),
            pltpu.VMEM((N_DIR, batch, STATE_LANES), F32),
        ],
        compiler_params=_params("parallel"),
        name="s5_mixer",
    )(*args)


def _mixout_kernel(y_ref, gb_ref, zc_ref, zp_ref, zn_ref, x_ref, mod_ref, wglu_ref, wout_ref,
                   cw_ref, cb_ref, g_ref, x1_ref, h2_ref, *, batch, seg):
    tm = x_ref.shape[0]
    tl = tm // batch
    l0 = pl.program_id(0) * tl
    prev_ok = (l0 % seg != 0).astype(F32)
    next_ok = ((l0 + tl) % seg != 0).astype(F32)

    y = jax.nn.gelu(y_ref[...])
    y_ssm = y * jax.nn.sigmoid(jnp.dot(y.astype(BF16), wglu_ref[0], preferred_element_type=F32))

    zc = zc_ref[...].astype(F32)
    zp = zp_ref[...].astype(F32)[HALO_ROWS - batch:, :]
    zn = zn_ref[...].astype(F32)[:batch, :]
    prev = jnp.concatenate([zp * prev_ok, zc[:tm - batch]], axis=0)
    nxt = jnp.concatenate([zc[batch:], zn * next_ok], axis=0)
    conv = prev * cw_ref[0, 0:1, :] + zc * cw_ref[0, 1:2, :] + nxt * cw_ref[0, 2:3, :] + cb_ref[0]
    y_conv = gb_ref[...].astype(F32) * conv

    out = jnp.dot(y_conv.astype(BF16), wout_ref[0, D_SSM:, :], preferred_element_type=F32)
    out += jnp.dot(y_ssm.astype(BF16), wout_ref[0, :D_SSM, :], preferred_element_type=F32)
    x1 = (_rows8(x_ref[...]) + mod_ref[0, 2][None] * _rows8(out)).reshape(tm, D_MODEL)
    x1_ref[...] = x1
    h2_ref[...] = _norm_mod(x1, g_ref[0], mod_ref[0, 4], mod_ref[0, 3]).astype(BF16)


def _mixout(y, gb, zc, x, mod, w_glu, w_out, conv_w, conv_b, g_ffn, batch, seg, layer):
    r = x.shape[0]
    tm = TM_MIXOUT
    assert seg % (tm // batch) == 0
    hb = tm // HALO_ROWS
    n_hb = r // HALO_ROWS
    row = lambda i: (i, 0)
    lay = lambda i: (layer, 0, 0)
    return pl.pallas_call(
        functools.partial(_mixout_kernel, batch=batch, seg=seg),
        grid=(r // tm,),
        in_specs=[
            pl.BlockSpec((tm, D_SSM), row),
            pl.BlockSpec((tm, D_CONV), row),
            pl.BlockSpec((tm, D_CONV), row),
            pl.BlockSpec((HALO_ROWS, D_CONV), lambda i: (jnp.maximum(i * hb - 1, 0), 0)),
            pl.BlockSpec((HALO_ROWS, D_CONV), lambda i: (jnp.minimum((i + 1) * hb, n_hb - 1), 0)),
            pl.BlockSpec((tm, D_MODEL), row),
            _resident((1, N_MOD, SUBLANES, D_MODEL), lambda i: (layer, 0, 0, 0)),
            _resident((1, D_SSM, D_SSM), lay),
            _resident((1, D_MODEL, D_MODEL), lay),
            _resident((1, 3, D_CONV), lay),
            _resident((1, 1, D_CONV), lay),
            _resident((1, 1, D_MODEL), lay),
        ],
        out_specs=[pl.BlockSpec((tm, D_MODEL), row), pl.BlockSpec((tm, D_MODEL), row)],
        out_shape=[jax.ShapeDtypeStruct((r, D_MODEL), F32), jax.ShapeDtypeStruct((r, D_MODEL), BF16)],
        compiler_params=_params("parallel"),
        name="mixer_out",
    )(y, gb, zc, zc, zc, x, mod, w_glu, w_out, conv_w, conv_b, g_ffn)


def _ffn_kernel(h_ref, x_ref, mod_ref, wgu_ref, wd_ref, gf_ref, o_ref, *scratch, final):
    k = pl.program_id(1)
    acc = scratch[0] if final else o_ref

    @pl.when(k == 0)
    def _():
        acc[...] = jnp.zeros_like(acc)

    gate_up = jnp.dot(h_ref[...], wgu_ref[0, 0], preferred_element_type=F32)
    hw = gate_up.shape[1] // 4
    act = jnp.concatenate(
        [jax.nn.silu(gate_up[:, 2 * s * hw:(2 * s + 1) * hw]) * gate_up[:, (2 * s + 1) * hw:(2 * s + 2) * hw]
         for s in range(2)], axis=1).astype(BF16)
    acc[...] += jnp.dot(act, wd_ref[0], preferred_element_type=F32)

    @pl.when(k == pl.num_programs(1) - 1)
    def _():
        x2 = (_rows8(x_ref[...]) + mod_ref[0, 5][None] * _rows8(acc[...])).reshape(x_ref.shape)
        if final:
            ms = jnp.mean(x2 * x2, axis=-1, keepdims=True)
            x2 = x2 * lax.rsqrt(ms + EPS) * gf_ref[...]
            b = o_ref.shape[0]
            for t in range(o_ref.shape[1]):
                o_ref[:, t, :] = x2[t * b:(t + 1) * b, :]
        else:
            o_ref[...] = x2


def _ffn(h2, x1, mod, w_gate_up, w_down, g_final, layer, batch, final):
    r = x1.shape[0]
    tf = TF_FFN
    if final:
        tm = TM_FFN
        out_spec = pl.BlockSpec((batch, tm // batch, D_MODEL), lambda i, k: (0, i, 0))
        out_shape = jax.ShapeDtypeStruct((batch, r // batch, D_MODEL), F32)
        scratch = [pltpu.VMEM((tm, D_MODEL), F32)]
    else:
        tm = TM_FFN_ACC
        out_spec = pl.BlockSpec((tm, D_MODEL), lambda i, k: (i, 0))
        out_shape = jax.ShapeDtypeStruct((r, D_MODEL), F32)
        scratch = []
    return pl.pallas_call(
        functools.partial(_ffn_kernel, final=final),
        grid=(r // tm, D_FF // tf),
        in_specs=[
            pl.BlockSpec((tm, D_MODEL), lambda i, k: (i, 0)),
            pl.BlockSpec((tm, D_MODEL), lambda i, k: (i, 0)),
            pl.BlockSpec((1, N_MOD, SUBLANES, D_MODEL), lambda i, k: (layer, 0, 0, 0)),
            pl.BlockSpec((1, 1, D_MODEL, 2 * tf), lambda i, k: (layer, k, 0, 0)),
            pl.BlockSpec((1, tf, D_MODEL), lambda i, k: (layer, k, 0)),
            pl.BlockSpec((1, D_MODEL), lambda i, k: (0, 0)),
        ],
        out_specs=out_spec,
        out_shape=out_shape,
        scratch_shapes=scratch,
        compiler_params=_params("parallel", "arbitrary"),
        name="ffn",
    )(h2, x1, mod, w_gate_up, w_down, g_final)


def _s5_build_kernel(pw_ref, bb_ref, cs_ref, bc_ref, cc_ref, toep_ref):
    row_group = lax.broadcasted_iota(jnp.int32, (LANES, LANES), 0) // SSM_H
    lane = lax.broadcasted_iota(jnp.int32, (LANES, LANES), 1)
    row_group_s = lax.broadcasted_iota(jnp.int32, (LANES, SSM_STATE), 0) // SSM_H

    def per_row(p):
        out = jnp.zeros((LANES, SSM_STATE), F32)
        for g in range(GROUPS_PER_BLOCK):
            out = jnp.where(row_group_s == g, p[g:g + 1, :], out)
        return out

    dup_state = (lax.broadcasted_iota(jnp.int32, (SSM_STATE, LANES), 1) % SSM_STATE
                 == lax.broadcasted_iota(jnp.int32, (SSM_STATE, LANES), 0)).astype(BF16)
    keep = [(row_group // 2 == q) & (lane // SSM_STATE == row_group % 2) for q in range(PAIRS)]
    over_states = (((1,), (1,)), ((), ()))
    taps = {}
    for d in range(N_DIR):
        b_re, b_im = bb_ref[0, d, 0, 0], bb_ref[0, d, 1, 0]
        c_re, c_im = cs_ref[0, d, 0, 0], cs_ref[0, d, 1, 0]
        for i in range(T_BLK):
            pr, pi = per_row(pw_ref[0, d, 0, i, 0, 0]), per_row(pw_ref[0, d, 0, i, 1, 0])
            qr, qi = per_row(pw_ref[0, d, 1, i, 0, 0]), per_row(pw_ref[0, d, 1, i, 1, 0])
            src_b = (b_re * pr - b_im * pi, b_re * pi + b_im * pr)
            src_c = (c_re * qr - c_im * qi, -(c_re * qi + c_im * qr))
            lag = T_BLK - 1 - i if d == 0 else i
            taps[d, lag] = (
                lax.dot_general(src_b[0], c_re, over_states, precision=lax.Precision.HIGHEST,
                                preferred_element_type=F32)
                - lax.dot_general(src_b[1], c_im, over_states, precision=lax.Precision.HIGHEST,
                                  preferred_element_type=F32))
            for r in range(2):
                wide_b = jnp.dot(src_b[r].astype(BF16), dup_state, preferred_element_type=F32)
                wide_c = jnp.dot(src_c[r].astype(BF16), dup_state, preferred_element_type=F32)
                for q in range(PAIRS):
                    c0 = d * STATE_LANES + r * (STATE_LANES // 2) + q * LANES
                    bc_ref[0, 0, i * LANES:(i + 1) * LANES, c0:c0 + LANES] = (
                        jnp.where(keep[q], wide_b, 0.0).astype(BF16))
                    cc_ref[0, 0, c0:c0 + LANES, i * LANES:(i + 1) * LANES] = (
                        jnp.where(keep[q], wide_c, 0.0).T.astype(BF16))
    diag = row_group == lane // SSM_H
    for i_in in range(T_BLK):
        for i_out in range(T_BLK):
            lag = i_out - i_in
            tap = taps[0, lag] if lag > 0 else taps[1, -lag] if lag < 0 else taps[0, 0] + taps[1, 0]
            toep_ref[0, 0, i_in * LANES:(i_in + 1) * LANES, i_out * LANES:(i_out + 1) * LANES] = (
                jnp.where(diag, tap, 0.0).astype(BF16))


def _cpow(lr, li, dt, k):
    m = jnp.exp(lr * dt * k)
    return m * jnp.cos(li * dt * k), m * jnp.sin(li * dt * k)


def _s5_sources(lam_re, lam_im, log_dt, b_re, b_im, c_re, c_im):
    t = T_BLK
    nb = N_GROUP_BLOCKS
    lr = jnp.minimum(lam_re, LAM_RE_MAX)
    li = lam_im
    dt = jnp.exp(log_dt)[..., None]
    ar, ai = _cpow(lr, li, dt, 1.0)
    den = lr * lr + li * li
    qr = (((ar - 1.0) * lr + ai * li) / den)[:, :, None, :]
    qi = ((ai * lr - (ar - 1.0) * li) / den)[:, :, None, :]
    b_re_t = b_re.transpose(0, 1, 3, 2)
    b_im_t = b_im.transpose(0, 1, 3, 2)
    bbr = qr * b_re_t - qi * b_im_t
    bbi = qr * b_im_t + qi * b_re_t

    steps = jnp.arange(t, dtype=F32)
    e_in = jnp.stack([t - 1.0 - steps, steps], axis=0)[:, :, None, None]
    e_out = jnp.stack([steps + 1.0, t - steps], axis=0)[:, :, None, None]
    lr4, li4, dt4 = lr[:, None], li[:, None], dt[:, None]
    pw = jnp.stack([jnp.stack(_cpow(lr4, li4, dt4, e_in), axis=2),
                    jnp.stack(_cpow(lr4, li4, dt4, e_out), axis=2)], axis=1)
    pw = pw.reshape(N_DIR, 2, t, 2, nb, GROUPS_PER_BLOCK, SSM_STATE)
    bbt = jnp.stack([bbr, bbi], axis=1).reshape(N_DIR, 2, nb, LANES, SSM_STATE)
    cs = jnp.stack([c_re, c_im], axis=1).reshape(N_DIR, 2, nb, LANES, SSM_STATE)

    atr, ati = _cpow(lr, li, dt, float(t))
    a = jnp.stack([atr, ati], axis=0)
    a = a.reshape(2, N_DIR, nb, PAIRS * LANES).transpose(2, 0, 1, 3).reshape(nb, 2, 1, N_DIR * PAIRS * LANES)
    a = jnp.broadcast_to(a, (nb, 2, SUBLANES, N_DIR * PAIRS * LANES))
    return pw, bbt, cs, a


def _s5_matrices(lam_re, lam_im, log_dt, b_re, b_im, c_re, c_im):
    t = T_BLK
    nb = N_GROUP_BLOCKS
    pw, bbt, cs, a = jax.vmap(_s5_sources)(lam_re, lam_im, log_dt, b_re, b_im, c_re, c_im)
    coef_spec = pl.BlockSpec((1, N_DIR, 2, 1, LANES, SSM_STATE), lambda l, g: (l, 0, 0, g, 0, 0))
    shape = lambda r, c: jax.ShapeDtypeStruct((DEPTH, nb, r, c), BF16)
    spec = lambda r, c: pl.BlockSpec((1, 1, r, c), lambda l, g: (l, g, 0, 0))
    bc, cc, toep = pl.pallas_call(
        _s5_build_kernel,
        grid=(DEPTH, nb),
        in_specs=[
            pl.BlockSpec((1, N_DIR, 2, t, 2, 1, GROUPS_PER_BLOCK, SSM_STATE), lambda l, g: (l, 0, 0, 0, 0, g, 0, 0)),
            coef_spec,
            coef_spec,
        ],
        out_specs=[spec(t * LANES, N_DIR * STATE_LANES), spec(N_DIR * STATE_LANES, t * LANES),
                   spec(t * LANES, t * LANES)],
        out_shape=[shape(t * LANES, N_DIR * STATE_LANES), shape(N_DIR * STATE_LANES, t * LANES),
                   shape(t * LANES, t * LANES)],
        compiler_params=_params("parallel", "parallel"),
        name="s5_build",
    )(pw, bbt, cs)
    return bc, cc, toep, a


def _init_states(st):
    b = st.shape[0]
    s = st.reshape(b, N_DIR, 2, N_GROUP_BLOCKS, STATE_LANES // 2)
    return s.transpose(3, 1, 0, 2, 4).reshape(N_GROUP_BLOCKS, N_DIR, b, STATE_LANES)


def _final_states(fin):
    b = fin.shape[2]
    s = fin.reshape(N_GROUP_BLOCKS, N_DIR, b, 2, STATE_LANES // 2)
    return s.transpose(2, 1, 3, 0, 4).reshape(b, N_DIR, 2, SSM_GROUPS, SSM_STATE)


def kernel(x_prompt, x_sample, state_ssm, c, c_ctx, w_ada, b_ada, g_mix, w_in, ssm_lam_re, ssm_lam_im, ssm_log_dt, ssm_b_re, ssm_b_im, ssm_c_re, ssm_c_im, ssm_d, w_glu, conv_w, conv_b, w_out, g_ffn, w_gate, w_up, w_down, g_final):
    n_ctx, l_ctx, _ = x_prompt.shape
    n_dec, l_dec, _ = x_sample.shape

    cond = jnp.zeros((16, D_MODEL), F32).at[0].set(c_ctx).at[1:1 + n_dec].set(c)
    mod = _modulation(cond, w_ada, b_ada).reshape(DEPTH, 16, N_MOD, D_MODEL)
    mod_ctx = jnp.broadcast_to(mod[:, 0, :, None, :], (DEPTH, N_MOD, SUBLANES, D_MODEL))
    mod_dec = mod[:, 1:1 + n_dec].transpose(0, 2, 1, 3)

    w_in_t = _cast_col_tiles(w_in, TN_INPROJ, parts=4)
    w_glu_b = _cast_rows(w_glu, 512)
    w_out_b = _cast_rows(w_out, 512)
    w_gate_up_b = _cast_pair_col_tiles(w_gate, w_up, TF_FFN)
    w_down_b = _cast_rows(w_down, TF_FFN)
    g_mix3 = g_mix[:, None, :]
    g_ffn3 = g_ffn[:, None, :]
    conv_b3 = conv_b[:, None, :]
    ssm_d3 = ssm_d[:, None, :]

    groups = [
        dict(x=x_prompt, batch=n_ctx, seg=l_ctx, mod=mod_ctx, s0=None),
        dict(x=x_sample, batch=n_dec, seg=GRID_W, mod=mod_dec, s0=state_ssm.astype(F32)),
    ]

    bc, cc, toep, a = _s5_matrices(ssm_lam_re, ssm_lam_im, ssm_log_dt, ssm_b_re, ssm_b_im, ssm_c_re, ssm_c_im)
    finals = []
    for l in range(DEPTH):
        for grp in groups:
            batch = grp["batch"]
            s0 = None if grp["s0"] is None else _init_states(grp["s0"][:, l])
            if l == 0:
                x, u, gb, zc = _inproj(grp["x"], grp["mod"], g_mix3, w_in_t, l, batch_major=True)
            else:
                x = grp["x"]
                u, gb, zc = _inproj(x, grp["mod"], g_mix3, w_in_t, l, batch_major=False)
            y, fin = _s5(u, bc, cc, toep, a, ssm_d3, s0, batch, l)
            x1, h2 = _mixout(y, gb, zc, x, grp["mod"], w_glu_b, w_out_b, conv_w, conv_b3, g_ffn3,
                             batch, grp["seg"], l)
            grp["x"] = _ffn(h2, x1, grp["mod"], w_gate_up_b, w_down_b, g_final[None], l, batch,
                            final=(l == DEPTH - 1))
            if grp["s0"] is None:
                finals.append(_final_states(fin))

    new_state = jnp.stack(finals, axis=1).astype(x_prompt.dtype)
    return (groups[0]["x"], groups[1]["x"], new_state)
```

```python
import functools

import jax
import jax.numpy as jnp
from jax import lax
from jax.experimental import pallas as pl
from jax.experimental.pallas import tpu as pltpu

D_MODEL = 2048
DEPTH = 2
GRID_W = 64
D_SSM = 1024
D_CONV = 1024
SSM_H = 16
SSM_GROUPS = 64
SSM_STATE = 64
N_DIR = 2
D_FF = 5632
N_MOD = 6
EPS = 1e-6
LAM_RE_MAX = -1e-4

SUBLANES = 8
LANES = 128
GROUPS_PER_BLOCK = LANES // SSM_H
N_GROUP_BLOCKS = SSM_GROUPS // GROUPS_PER_BLOCK
PAIRS = GROUPS_PER_BLOCK // 2
STATE_LANES = GROUPS_PER_BLOCK * 2 * SSM_STATE
SCAN_ROWS = 1024
HALO_ROWS = 16
T_BLK = 4
VMEM_LIMIT = 58 * 1024 * 1024

TM_INPROJ = 1024
TM_MIXOUT = 512
TM_FFN = 512
TM_FFN_ACC = 1024
TN_INPROJ = 1024
TF_FFN = 512
RING_SLOTS = 3

F32 = jnp.float32
BF16 = jnp.bfloat16


def _params(*sem):
    return pltpu.CompilerParams(dimension_semantics=sem, vmem_limit_bytes=VMEM_LIMIT)


def _resident(shape, index_map):
    return pl.BlockSpec(shape, index_map, pipeline_mode=pl.Buffered(1))


def _rows8(a):
    return a.reshape(a.shape[0] // SUBLANES, SUBLANES, a.shape[1])


def _norm_mod(x, gain, scale, shift):
    ms = jnp.mean(x * x, axis=-1, keepdims=True)
    y = x * lax.rsqrt(ms + EPS) * gain
    h = _rows8(y) * (1.0 + scale)[None] + shift[None]
    return h.reshape(x.shape)


def _cast_kernel(w_ref, o_ref):
    o_ref[...] = w_ref[...].astype(BF16).reshape(o_ref.shape)


def _cast_col_tiles(w, tn, parts=1):
    _, k, n = w.shape
    nt = n // tn
    q = tn // parts
    return pl.pallas_call(
        _cast_kernel,
        grid=(DEPTH, nt, parts),
        in_specs=[pl.BlockSpec((1, k, q), lambda l, j, p: (l, 0, p * nt + j))],
        out_specs=pl.BlockSpec((1, 1, k, q), lambda l, j, p: (l, j, 0, p)),
        out_shape=jax.ShapeDtypeStruct((DEPTH, nt, k, tn), BF16),
        compiler_params=_params("parallel", "parallel", "parallel"),
        name="cast_col_tiles",
    )(w)


def _cast_pair_kernel(a_ref, b_ref, o_ref):
    half = a_ref.shape[2] // 2
    for s in range(2):
        o_ref[0, 0, :, (2 * s) * half:(2 * s + 1) * half] = a_ref[0, :, s * half:(s + 1) * half].astype(BF16)
        o_ref[0, 0, :, (2 * s + 1) * half:(2 * s + 2) * half] = b_ref[0, :, s * half:(s + 1) * half].astype(BF16)


def _cast_pair_col_tiles(wa, wb, tn):
    _, k, n = wa.shape
    spec = pl.BlockSpec((1, k, tn), lambda l, j: (l, 0, j))
    return pl.pallas_call(
        _cast_pair_kernel,
        grid=(DEPTH, n // tn),
        in_specs=[spec, spec],
        out_specs=pl.BlockSpec((1, 1, k, 2 * tn), lambda l, j: (l, j, 0, 0)),
        out_shape=jax.ShapeDtypeStruct((DEPTH, n // tn, k, 2 * tn), BF16),
        compiler_params=_params("parallel", "parallel"),
        name="cast_pair_col_tiles",
    )(wa, wb)


def _cast_rows(w, tk):
    _, k, n = w.shape
    return pl.pallas_call(
        _cast_kernel,
        grid=(DEPTH, k // tk),
        in_specs=[pl.BlockSpec((1, tk, n), lambda l, i: (l, i, 0))],
        out_specs=pl.BlockSpec((1, tk, n), lambda l, i: (l, i, 0)),
        out_shape=jax.ShapeDtypeStruct((DEPTH, k, n), BF16),
        compiler_params=_params("parallel", "parallel"),
        name="cast_rows",
    )(w)


def _mod_kernel(cond_ref, w_ref, b_ref, o_ref):
    s = jax.nn.silu(cond_ref[...]).astype(BF16)
    o_ref[0] = jnp.dot(s, w_ref[0].astype(BF16), preferred_element_type=F32) + b_ref[0]


def _modulation(cond, w_ada, b_ada):
    tn = 1024
    n = N_MOD * D_MODEL
    return pl.pallas_call(
        _mod_kernel,
        grid=(DEPTH, n // tn),
        in_specs=[
            pl.BlockSpec((16, D_MODEL), lambda l, j: (0, 0)),
            pl.BlockSpec((1, D_MODEL, tn), lambda l, j: (l, 0, j)),
            pl.BlockSpec((1, 1, tn), lambda l, j: (l, 0, j)),
        ],
        out_specs=pl.BlockSpec((1, 16, tn), lambda l, j: (l, 0, j)),
        out_shape=jax.ShapeDtypeStruct((DEPTH, 16, n), F32),
        compiler_params=_params("parallel", "parallel"),
        name="adaln_modulation",
    )(cond, w_ada, b_ada.reshape(DEPTH, 1, n))


def _inproj_kernel(x_ref, mod_ref, g_ref, w_ref, *out_refs, batch_major):
    if batch_major:
        xt_ref, u_ref, gb_ref, zc_ref, h_even, h_odd = out_refs
    else:
        u_ref, gb_ref, zc_ref, h_even, h_odd = out_refs
    i = pl.program_id(0)
    j = pl.program_id(1)
    tq = h_even.shape[0] // pl.num_programs(1)

    def prepare(h_next):
        if batch_major:
            x = jnp.concatenate([x_ref[:, t, :] for t in range(x_ref.shape[1])], axis=0)
            xt_ref[...] = x
        else:
            x = x_ref[...]
        h = _norm_mod(x, g_ref[0], mod_ref[0, 1], mod_ref[0, 0])
        h_next[pl.ds(pl.multiple_of(j * tq, tq), tq), :] = h.astype(BF16)

    def multiply(h_cur):
        z = jnp.dot(h_cur[...], w_ref[0, 0], preferred_element_type=F32)
        q = z.shape[1] // 4
        u_ref[...] = z[:, :q]
        gb_ref[...] = z[:, q:2 * q].astype(gb_ref.dtype)
        zc_ref[...] = (z[:, 2 * q:3 * q] * z[:, 3 * q:]).astype(zc_ref.dtype)

    n = pl.num_programs(0) - 1
    bufs = (h_even, h_odd)

    @pl.when(i == 0)
    def _():
        prepare(h_even)

    for parity in range(2):
        @pl.when((i > 0) & (i < n) & (i % 2 == parity))
        def _():
            prepare(bufs[parity])
            multiply(bufs[1 - parity])

        @pl.when((i == n) & (i % 2 == parity))
        def _():
            multiply(bufs[1 - parity])


def _inproj(x, mod, g_mix, w_in_tiled, layer, batch_major):
    tm, tn = TM_INPROJ, TN_INPROJ
    q = tn // 4
    nj = 4 * D_SSM // tn
    tq = tm // nj
    r = x.shape[0] * x.shape[1] if batch_major else x.shape[0]
    n = r // tm
    quarter = lambda i, j: jnp.minimum(i * nj + j, n * nj - 1)
    if batch_major:
        b = x.shape[0]
        x_spec = pl.BlockSpec((b, tq // b, D_MODEL), lambda i, j: (0, quarter(i, j), 0))
    else:
        x_spec = pl.BlockSpec((tq, D_MODEL), lambda i, j: (quarter(i, j), 0))
    out = jax.ShapeDtypeStruct((r, D_SSM), F32)
    out_specs = [pl.BlockSpec((tm, q), lambda i, j: (jnp.maximum(i - 1, 0), jnp.where(i == 0, 0, j)))] * 3
    gate = jax.ShapeDtypeStruct((r, D_CONV), BF16)
    out_shape = [out, gate, gate]
    if batch_major:
        out_specs = [pl.BlockSpec((tq, D_MODEL), lambda i, j: (quarter(i, j), 0))] + out_specs
        out_shape = [jax.ShapeDtypeStruct((r, D_MODEL), F32)] + out_shape
    return pl.pallas_call(
        functools.partial(_inproj_kernel, batch_major=batch_major),
        grid=(n + 1, nj),
        in_specs=[
            x_spec,
            pl.BlockSpec((1, N_MOD, SUBLANES, D_MODEL), lambda i, j: (layer, 0, 0, 0)),
            pl.BlockSpec((1, 1, D_MODEL), lambda i, j: (layer, 0, 0)),
            pl.BlockSpec((1, 1, D_MODEL, tn), lambda i, j: (layer, j, 0, 0)),
        ],
        out_specs=out_specs,
        out_shape=out_shape,
        scratch_shapes=[pltpu.VMEM((tm, D_MODEL), BF16), pltpu.VMEM((tm, D_MODEL), BF16)],
        compiler_params=_params("arbitrary", "arbitrary"),
        name="inproj",
    )(x, mod, g_mix, w_in_tiled)


def _s5_kernel(*refs, batch, has_init):
    if has_init:
        u_ref, bc_ref, cc_ref, toep_ref, a_ref, d_ref, s0_ref, y_ref, fin_ref, sf, sb, xs = refs
    else:
        u_ref, bc_ref, cc_ref, toep_ref, a_ref, d_ref, y_ref, fin_ref, sf, sb, xs = refs
        s0_ref = None
    rows = u_ref.shape[0]
    rc = SCAN_ROWS
    n_chunks = rows // rc
    nblk = rc // (batch * T_BLK)
    rblk = nblk * batch
    halves = batch // SUBLANES
    tl = T_BLK * LANES

    y_ref[...] = d_ref[0] * u_ref[...]
    if has_init:
        xs[...] = s0_ref[0]
    else:
        xs[...] = jnp.zeros_like(xs)

    def gather(r0):
        uc = u_ref[pl.ds(r0, rc), :].reshape(nblk, T_BLK, batch, LANES)
        return jnp.concatenate([uc[:, i].reshape(rblk, LANES) for i in range(T_BLK)], axis=1).astype(BF16)

    def scatter_add(r0, yb):
        parts = [yb[:, i * LANES:(i + 1) * LANES].reshape(nblk, 1, batch, LANES) for i in range(T_BLK)]
        y_ref[pl.ds(r0, rc), :] += jnp.concatenate(parts, axis=1).reshape(rc, LANES)

    def chunk(ci, carry):
        rf = pl.multiple_of(ci * rc, rc)
        rb = pl.multiple_of((n_chunks - 1 - ci) * rc, rc)
        lf = gather(rf)
        lb = gather(rb)
        sf[...] = jnp.dot(lf, bc_ref[0, 0, :, :STATE_LANES], preferred_element_type=F32)
        sb[...] = jnp.dot(lb, bc_ref[0, 0, :, STATE_LANES:], preferred_element_type=F32)
        for half in range(halves):
            hs = slice(half * SUBLANES, (half + 1) * SUBLANES)
            for pair in range(PAIRS):
                re = slice(pair * LANES, (pair + 1) * LANES)
                im = slice(STATE_LANES // 2 + pair * LANES, STATE_LANES // 2 + (pair + 1) * LANES)
                for d, s in ((0, sf), (1, sb)):
                    al = slice(d * PAIRS * LANES + pair * LANES, d * PAIRS * LANES + (pair + 1) * LANES)
                    ar = a_ref[0, 0, 0, :, al]
                    ai = a_ref[0, 0, 1, :, al]
                    xr = xs[d, hs, re]
                    xi = xs[d, hs, im]
                    for t in range(nblk):
                        blk = t if d == 0 else nblk - 1 - t
                        rs = slice(blk * batch + half * SUBLANES, blk * batch + (half + 1) * SUBLANES)
                        sr = s[rs, re]
                        si = s[rs, im]
                        s[rs, re] = xr
                        s[rs, im] = xi
                        xr, xi = ar * xr - ai * xi + sr, ar * xi + ai * xr + si
                    xs[d, hs, re] = xr
                    xs[d, hs, im] = xi
        yf = jnp.dot(sf[...].astype(BF16), cc_ref[0, 0, :STATE_LANES, :], preferred_element_type=F32)
        yf += jnp.dot(lf, toep_ref[0, 0], preferred_element_type=F32)
        yb = jnp.dot(sb[...].astype(BF16), cc_ref[0, 0, STATE_LANES:, :], preferred_element_type=F32)
        scatter_add(rf, yf)
        scatter_add(rb, yb)
        return carry

    lax.fori_loop(0, n_chunks, chunk, 0)
    fin_ref[0] = xs[...]


def _s5(u, bc, cc, toep, a, d_skip, s0, batch, layer):
    r = u.shape[0]
    rblk = SCAN_ROWS // T_BLK
    tl = T_BLK * LANES
    has_init = s0 is not None
    in_specs = [
        pl.BlockSpec((r, LANES), lambda g: (0, g)),
        pl.BlockSpec((1, 1, tl, N_DIR * STATE_LANES), lambda g: (layer, g, 0, 0)),
        pl.BlockSpec((1, 1, N_DIR * STATE_LANES, tl), lambda g: (layer, g, 0, 0)),
        pl.BlockSpec((1, 1, tl, tl), lambda g: (layer, g, 0, 0)),
        pl.BlockSpec((1, 1, 2, SUBLANES, N_DIR * PAIRS * LANES), lambda g: (layer, g, 0, 0, 0)),
        pl.BlockSpec((1, 1, LANES), lambda g: (layer, 0, g)),
    ]
    args = [u, bc, cc, toep, a, d_skip]
    if has_init:
        in_specs.append(pl.BlockSpec((1, N_DIR, batch, STATE_LANES), lambda g: (g, 0, 0, 0)))
        args.append(s0)
    return pl.pallas_call(
        functools.partial(_s5_kernel, batch=batch, has_init=has_init),
        grid=(N_GROUP_BLOCKS,),
        in_specs=in_specs,
        out_specs=[
            pl.BlockSpec((r, LANES), lambda g: (0, g)),
            pl.BlockSpec((1, N_DIR, batch, STATE_LANES), lambda g: (g, 0, 0, 0)),
        ],
        out_shape=[
            jax.ShapeDtypeStruct((r, D_SSM), F32),
            jax.ShapeDtypeStruct((N_GROUP_BLOCKS, N_DIR, batch, STATE_LANES), F32),
        ],
        scratch_shapes=[
            pltpu.VMEM((rblk, STATE_LANES), F32),
            pltpu.VMEM((rblk, STATE_LANES), F32),
            pltpu.VMEM((N_DIR, batch, STATE_LANES), F32),
        ],
        compiler_params=_params("parallel"),
        name="s5_mixer",
    )(*args)


def _mixout_kernel(y_ref, gb_ref, zc_ref, zp_ref, zn_ref, x_ref, mod_ref, wglu_ref, wout_ref,
                   cw_ref, cb_ref, g_ref, x1_ref, h2_ref, *, batch, seg):
    tm = x_ref.shape[0]
    tl = tm // batch
    l0 = pl.program_id(0) * tl
    prev_ok = (l0 % seg != 0).astype(F32)
    next_ok = ((l0 + tl) % seg != 0).astype(F32)

    y = jax.nn.gelu(y_ref[...])
    y_ssm = y * jax.nn.sigmoid(jnp.dot(y.astype(BF16), wglu_ref[0], preferred_element_type=F32))

    zc = zc_ref[...].astype(F32)
    zp = zp_ref[...].astype(F32)[HALO_ROWS - batch:, :]
    zn = zn_ref[...].astype(F32)[:batch, :]
    prev = jnp.concatenate([zp * prev_ok, zc[:tm - batch]], axis=0)
    nxt = jnp.concatenate([zc[batch:], zn * next_ok], axis=0)
    conv = prev * cw_ref[0, 0:1, :] + zc * cw_ref[0, 1:2, :] + nxt * cw_ref[0, 2:3, :] + cb_ref[0]
    y_conv = gb_ref[...].astype(F32) * conv

    out = jnp.dot(y_conv.astype(BF16), wout_ref[0, D_SSM:, :], preferred_element_type=F32)
    out += jnp.dot(y_ssm.astype(BF16), wout_ref[0, :D_SSM, :], preferred_element_type=F32)
    x1 = (_rows8(x_ref[...]) + mod_ref[0, 2][None] * _rows8(out)).reshape(tm, D_MODEL)
    x1_ref[...] = x1
    h2_ref[...] = _norm_mod(x1, g_ref[0], mod_ref[0, 4], mod_ref[0, 3]).astype(BF16)


def _mixout(y, gb, zc, x, mod, w_glu, w_out, conv_w, conv_b, g_ffn, batch, seg, layer):
    r = x.shape[0]
    tm = TM_MIXOUT
    assert seg % (tm // batch) == 0
    hb = tm // HALO_ROWS
    n_hb = r // HALO_ROWS
    row = lambda i: (i, 0)
    lay = lambda i: (layer, 0, 0)
    return pl.pallas_call(
        functools.partial(_mixout_kernel, batch=batch, seg=seg),
        grid=(r // tm,),
        in_specs=[
            pl.BlockSpec((tm, D_SSM), row),
            pl.BlockSpec((tm, D_CONV), row),
            pl.BlockSpec((tm, D_CONV), row),
            pl.BlockSpec((HALO_ROWS, D_CONV), lambda i: (jnp.maximum(i * hb - 1, 0), 0)),
            pl.BlockSpec((HALO_ROWS, D_CONV), lambda i: (jnp.minimum((i + 1) * hb, n_hb - 1), 0)),
            pl.BlockSpec((tm, D_MODEL), row),
            _resident((1, N_MOD, SUBLANES, D_MODEL), lambda i: (layer, 0, 0, 0)),
            _resident((1, D_SSM, D_SSM), lay),
            _resident((1, D_MODEL, D_MODEL), lay),
            _resident((1, 3, D_CONV), lay),
            _resident((1, 1, D_CONV), lay),
            _resident((1, 1, D_MODEL), lay),
        ],
        out_specs=[pl.BlockSpec((tm, D_MODEL), row), pl.BlockSpec((tm, D_MODEL), row)],
        out_shape=[jax.ShapeDtypeStruct((r, D_MODEL), F32), jax.ShapeDtypeStruct((r, D_MODEL), BF16)],
        compiler_params=_params("parallel"),
        name="mixer_out",
    )(y, gb, zc, zc, zc, x, mod, w_glu, w_out, conv_w, conv_b, g_ffn)


def _ffn_kernel(h_ref, x_ref, mod_ref, wgu_ref, wd_ref, gf_ref, o_ref, *scratch, final, layer):
    k = pl.program_id(1)
    if final:
        acc, wgu_buf, wd_buf, sem = scratch
        nk = pl.num_programs(1)
        total = pl.num_programs(0) * nk
        step = pl.program_id(0) * nk + k

        def copies(s):
            slot = s % RING_SLOTS
            return (pltpu.make_async_copy(wgu_ref.at[layer, s % nk], wgu_buf.at[slot], sem.at[0, slot]),
                    pltpu.make_async_copy(wd_ref.at[layer, s % nk], wd_buf.at[slot], sem.at[1, slot]))

        @pl.when(step == 0)
        def _():
            for s in range(RING_SLOTS - 1):
                for c in copies(s):
                    c.start()

        @pl.when(step + RING_SLOTS - 1 < total)
        def _():
            for c in copies(step + RING_SLOTS - 1):
                c.start()

        for c in copies(step):
            c.wait()
        w_gate_up = lambda: wgu_buf[step % RING_SLOTS]
        w_down = lambda: wd_buf[step % RING_SLOTS]
    else:
        acc = o_ref
        w_gate_up = lambda: wgu_ref[0, 0]
        w_down = lambda: wd_ref[0, 0]

    @pl.when(k == 0)
    def _():
        acc[...] = jnp.zeros_like(acc)

    gate_up = jnp.dot(h_ref[...], w_gate_up(), preferred_element_type=F32)
    hw = gate_up.shape[1] // 4
    act = jnp.concatenate(
        [jax.nn.silu(gate_up[:, 2 * s * hw:(2 * s + 1) * hw]) * gate_up[:, (2 * s + 1) * hw:(2 * s + 2) * hw]
         for s in range(2)], axis=1).astype(BF16)
    acc[...] += jnp.dot(act, w_down(), preferred_element_type=F32)

    @pl.when(k == pl.num_programs(1) - 1)
    def _():
        x2 = (_rows8(x_ref[...]) + mod_ref[0, 5][None] * _rows8(acc[...])).reshape(x_ref.shape)
        if final:
            ms = jnp.mean(x2 * x2, axis=-1, keepdims=True)
            x2 = x2 * lax.rsqrt(ms + EPS) * gf_ref[...]
            b = o_ref.shape[0]
            for t in range(o_ref.shape[1]):
                o_ref[:, t, :] = x2[t * b:(t + 1) * b, :]
        else:
            o_ref[...] = x2


def _ffn(h2, x1, mod, w_gate_up, w_down, g_final, layer, batch, final):
    r = x1.shape[0]
    tf = TF_FFN
    w_down = w_down.reshape(DEPTH, D_FF // tf, tf, D_MODEL)
    if final:
        tm = TM_FFN
        out_spec = pl.BlockSpec((batch, tm // batch, D_MODEL), lambda i, k: (0, i, 0))
        out_shape = jax.ShapeDtypeStruct((batch, r // batch, D_MODEL), F32)
        scratch = [pltpu.VMEM((tm, D_MODEL), F32),
                   pltpu.VMEM((RING_SLOTS, D_MODEL, 2 * tf), BF16),
                   pltpu.VMEM((RING_SLOTS, tf, D_MODEL), BF16),
                   pltpu.SemaphoreType.DMA((2, RING_SLOTS))]
        w_specs = [pl.BlockSpec(memory_space=pl.ANY), pl.BlockSpec(memory_space=pl.ANY)]
    else:
        tm = TM_FFN_ACC
        out_spec = pl.BlockSpec((tm, D_MODEL), lambda i, k: (i, 0))
        out_shape = jax.ShapeDtypeStruct((r, D_MODEL), F32)
        scratch = []
        w_specs = [pl.BlockSpec((1, 1, D_MODEL, 2 * tf), lambda i, k: (layer, k, 0, 0)),
                   pl.BlockSpec((1, 1, tf, D_MODEL), lambda i, k: (layer, k, 0, 0))]
    return pl.pallas_call(
        functools.partial(_ffn_kernel, final=final, layer=layer),
        grid=(r // tm, D_FF // tf),
        in_specs=[
            pl.BlockSpec((tm, D_MODEL), lambda i, k: (i, 0)),
            pl.BlockSpec((tm, D_MODEL), lambda i, k: (i, 0)),
            pl.BlockSpec((1, N_MOD, SUBLANES, D_MODEL), lambda i, k: (layer, 0, 0, 0)),
            *w_specs,
            pl.BlockSpec((1, D_MODEL), lambda i, k: (0, 0)),
        ],
        out_specs=out_spec,
        out_shape=out_shape,
        scratch_shapes=scratch,
        compiler_params=_params("arbitrary", "arbitrary"),
        name="ffn",
    )(h2, x1, mod, w_gate_up, w_down, g_final)


def _s5_build_kernel(pw_ref, bb_ref, cs_ref, bc_ref, cc_ref, toep_ref):
    row_group = lax.broadcasted_iota(jnp.int32, (LANES, LANES), 0) // SSM_H
    lane = lax.broadcasted_iota(jnp.int32, (LANES, LANES), 1)
    row_group_s = lax.broadcasted_iota(jnp.int32, (LANES, SSM_STATE), 0) // SSM_H

    def per_row(p):
        out = jnp.zeros((LANES, SSM_STATE), F32)
        for g in range(GROUPS_PER_BLOCK):
            out = jnp.where(row_group_s == g, p[g:g + 1, :], out)
        return out

    dup_state = (lax.broadcasted_iota(jnp.int32, (SSM_STATE, LANES), 1) % SSM_STATE
                 == lax.broadcasted_iota(jnp.int32, (SSM_STATE, LANES), 0)).astype(BF16)
    keep = [(row_group // 2 == q) & (lane // SSM_STATE == row_group % 2) for q in range(PAIRS)]
    over_states = (((1,), (1,)), ((), ()))
    taps = {}
    for d in range(N_DIR):
        b_re, b_im = bb_ref[0, d, 0, 0], bb_ref[0, d, 1, 0]
        c_re, c_im = cs_ref[0, d, 0, 0], cs_ref[0, d, 1, 0]
        for i in range(T_BLK):
            pr, pi = per_row(pw_ref[0, d, 0, i, 0, 0]), per_row(pw_ref[0, d, 0, i, 1, 0])
            qr, qi = per_row(pw_ref[0, d, 1, i, 0, 0]), per_row(pw_ref[0, d, 1, i, 1, 0])
            src_b = (b_re * pr - b_im * pi, b_re * pi + b_im * pr)
            src_c = (c_re * qr - c_im * qi, -(c_re * qi + c_im * qr))
            lag = T_BLK - 1 - i if d == 0 else i
            taps[d, lag] = (
                lax.dot_general(src_b[0], c_re, over_states, precision=lax.Precision.HIGHEST,
                                preferred_element_type=F32)
                - lax.dot_general(src_b[1], c_im, over_states, precision=lax.Precision.HIGHEST,
                                  preferred_element_type=F32))
            for r in range(2):
                wide_b = jnp.dot(src_b[r].astype(BF16), dup_state, preferred_element_type=F32)
                wide_c = jnp.dot(src_c[r].astype(BF16), dup_state, preferred_element_type=F32)
                for q in range(PAIRS):
                    c0 = d * STATE_LANES + r * (STATE_LANES // 2) + q * LANES
                    bc_ref[0, 0, i * LANES:(i + 1) * LANES, c0:c0 + LANES] = (
                        jnp.where(keep[q], wide_b, 0.0).astype(BF16))
                    cc_ref[0, 0, c0:c0 + LANES, i * LANES:(i + 1) * LANES] = (
                        jnp.where(keep[q], wide_c, 0.0).T.astype(BF16))
    diag = row_group == lane // SSM_H
    for i_in in range(T_BLK):
        for i_out in range(T_BLK):
            lag = i_out - i_in
            tap = taps[0, lag] if lag > 0 else taps[1, -lag] if lag < 0 else taps[0, 0] + taps[1, 0]
            toep_ref[0, 0, i_in * LANES:(i_in + 1) * LANES, i_out * LANES:(i_out + 1) * LANES] = (
                jnp.where(diag, tap, 0.0).astype(BF16))


def _cpow(lr, li, dt, k):
    m = jnp.exp(lr * dt * k)
    return m * jnp.cos(li * dt * k), m * jnp.sin(li * dt * k)


def _s5_sources(lam_re, lam_im, log_dt, b_re, b_im, c_re, c_im):
    t = T_BLK
    nb = N_GROUP_BLOCKS
    lr = jnp.minimum(lam_re, LAM_RE_MAX)
    li = lam_im
    dt = jnp.exp(log_dt)[..., None]
    ar, ai = _cpow(lr, li, dt, 1.0)
    den = lr * lr + li * li
    qr = (((ar - 1.0) * lr + ai * li) / den)[:, :, None, :]
    qi = ((ai * lr - (ar - 1.0) * li) / den)[:, :, None, :]
    b_re_t = b_re.transpose(0, 1, 3, 2)
    b_im_t = b_im.transpose(0, 1, 3, 2)
    bbr = qr * b_re_t - qi * b_im_t
    bbi = qr * b_im_t + qi * b_re_t

    steps = jnp.arange(t, dtype=F32)
    e_in = jnp.stack([t - 1.0 - steps, steps], axis=0)[:, :, None, None]
    e_out = jnp.stack([steps + 1.0, t - steps], axis=0)[:, :, None, None]
    lr4, li4, dt4 = lr[:, None], li[:, None], dt[:, None]
    pw = jnp.stack([jnp.stack(_cpow(lr4, li4, dt4, e_in), axis=2),
                    jnp.stack(_cpow(lr4, li4, dt4, e_out), axis=2)], axis=1)
    pw = pw.reshape(N_DIR, 2, t, 2, nb, GROUPS_PER_BLOCK, SSM_STATE)
    bbt = jnp.stack([bbr, bbi], axis=1).reshape(N_DIR, 2, nb, LANES, SSM_STATE)
    cs = jnp.stack([c_re, c_im], axis=1).reshape(N_DIR, 2, nb, LANES, SSM_STATE)

    atr, ati = _cpow(lr, li, dt, float(t))
    a = jnp.stack([atr, ati], axis=0)
    a = a.reshape(2, N_DIR, nb, PAIRS * LANES).transpose(2, 0, 1, 3).reshape(nb, 2, 1, N_DIR * PAIRS * LANES)
    a = jnp.broadcast_to(a, (nb, 2, SUBLANES, N_DIR * PAIRS * LANES))
    return pw, bbt, cs, a


def _s5_matrices(lam_re, lam_im, log_dt, b_re, b_im, c_re, c_im):
    t = T_BLK
    nb = N_GROUP_BLOCKS
    pw, bbt, cs, a = jax.vmap(_s5_sources)(lam_re, lam_im, log_dt, b_re, b_im, c_re, c_im)
    coef_spec = pl.BlockSpec((1, N_DIR, 2, 1, LANES, SSM_STATE), lambda l, g: (l, 0, 0, g, 0, 0))
    shape = lambda r, c: jax.ShapeDtypeStruct((DEPTH, nb, r, c), BF16)
    spec = lambda r, c: pl.BlockSpec((1, 1, r, c), lambda l, g: (l, g, 0, 0))
    bc, cc, toep = pl.pallas_call(
        _s5_build_kernel,
        grid=(DEPTH, nb),
        in_specs=[
            pl.BlockSpec((1, N_DIR, 2, t, 2, 1, GROUPS_PER_BLOCK, SSM_STATE), lambda l, g: (l, 0, 0, 0, 0, g, 0, 0)),
            coef_spec,
            coef_spec,
        ],
        out_specs=[spec(t * LANES, N_DIR * STATE_LANES), spec(N_DIR * STATE_LANES, t * LANES),
                   spec(t * LANES, t * LANES)],
        out_shape=[shape(t * LANES, N_DIR * STATE_LANES), shape(N_DIR * STATE_LANES, t * LANES),
                   shape(t * LANES, t * LANES)],
        compiler_params=_params("parallel", "parallel"),
        name="s5_build",
    )(pw, bbt, cs)
    return bc, cc, toep, a


def _init_states(st):
    b = st.shape[0]
    s = st.reshape(b, N_DIR, 2, N_GROUP_BLOCKS, STATE_LANES // 2)
    return s.transpose(3, 1, 0, 2, 4).reshape(N_GROUP_BLOCKS, N_DIR, b, STATE_LANES)


def _final_states(fin):
    b = fin.shape[2]
    s = fin.reshape(N_GROUP_BLOCKS, N_DIR, b, 2, STATE_LANES // 2)
    return s.transpose(2, 1, 3, 0, 4).reshape(b, N_DIR, 2, SSM_GROUPS, SSM_STATE)


def kernel(x_prompt, x_sample, state_ssm, c, c_ctx, w_ada, b_ada, g_mix, w_in, ssm_lam_re, ssm_lam_im, ssm_log_dt, ssm_b_re, ssm_b_im, ssm_c_re, ssm_c_im, ssm_d, w_glu, conv_w, conv_b, w_out, g_ffn, w_gate, w_up, w_down, g_final):
    n_ctx, l_ctx, _ = x_prompt.shape
    n_dec, l_dec, _ = x_sample.shape

    cond = jnp.zeros((16, D_MODEL), F32).at[0].set(c_ctx).at[1:1 + n_dec].set(c)
    mod = _modulation(cond, w_ada, b_ada).reshape(DEPTH, 16, N_MOD, D_MODEL)
    mod_ctx = jnp.broadcast_to(mod[:, 0, :, None, :], (DEPTH, N_MOD, SUBLANES, D_MODEL))
    mod_dec = mod[:, 1:1 + n_dec].transpose(0, 2, 1, 3)

    w_in_t = _cast_col_tiles(w_in, TN_INPROJ, parts=4)
    w_glu_b = _cast_rows(w_glu, 512)
    w_out_b = _cast_rows(w_out, 512)
    w_gate_up_b = _cast_pair_col_tiles(w_gate, w_up, TF_FFN)
    w_down_b = _cast_rows(w_down, TF_FFN)
    g_mix3 = g_mix[:, None, :]
    g_ffn3 = g_ffn[:, None, :]
    conv_b3 = conv_b[:, None, :]
    ssm_d3 = ssm_d[:, None, :]

    groups = [
        dict(x=x_prompt, batch=n_ctx, seg=l_ctx, mod=mod_ctx, s0=None),
        dict(x=x_sample, batch=n_dec, seg=GRID_W, mod=mod_dec, s0=state_ssm.astype(F32)),
    ]

    bc, cc, toep, a = _s5_matrices(ssm_lam_re, ssm_lam_im, ssm_log_dt, ssm_b_re, ssm_b_im, ssm_c_re, ssm_c_im)
    finals = []
    for l in range(DEPTH):
        for grp in groups:
            batch = grp["batch"]
            s0 = None if grp["s0"] is None else _init_states(grp["s0"][:, l])
            if l == 0:
                x, u, gb, zc = _inproj(grp["x"], grp["mod"], g_mix3, w_in_t, l, batch_major=True)
            else:
                x = grp["x"]
                u, gb, zc = _inproj(x, grp["mod"], g_mix3, w_in_t, l, batch_major=False)
            y, fin = _s5(u, bc, cc, toep, a, ssm_d3, s0, batch, l)
            x1, h2 = _mixout(y, gb, zc, x, grp["mod"], w_glu_b, w_out_b, conv_w, conv_b3, g_ffn3,
                             batch, grp["seg"], l)
            grp["x"] = _ffn(h2, x1, grp["mod"], w_gate_up_b, w_down_b, g_final[None], l, batch,
                            final=(l == DEPTH - 1))
            if grp["s0"] is None:
                finals.append(_final_states(fin))

    new_state = jnp.stack(finals, axis=1).astype(x_prompt.dtype)
    return (groups[0]["x"], groups[1]["x"], new_state)
```
